```python
import math
import jax, jax.numpy as jnp
from jax import lax
import numpy as np

D_MODEL = 2048
BATCH = 8
SEQ = 4096
DEPTH = 4

CHUNK = 64
Q_BLOCK = 128

D_DIFF = D_MODEL // 2
DIFF_HEAD_DIM = 128
DIFF_QK_HALF = DIFF_HEAD_DIM // 2
N_DIFF_HEADS = D_DIFF // DIFF_HEAD_DIM
DIFF_SCALE = 1.0 / math.sqrt(DIFF_QK_HALF)

D_POOL = D_MODEL // 4
POOL_WINDOWS = (2, 4, 8, 16)
N_POOL_GROUPS = len(POOL_WINDOWS)
POOL_GROUP = D_POOL // N_POOL_GROUPS

CONV_CH = D_MODEL // 4
CONV_K = 31

D_IN = 3 * D_DIFF + D_POOL + 2 * CONV_CH
D_MIX = D_DIFF + D_POOL + CONV_CH

D_FF = ((8 * D_MODEL // 3 + 127) // 128) * 128
FFN_CONV_K = 3

RMS_EPS = 1e-6
LN_EPS = 1e-5

kernel_name = "hybrid_diffattn_pool_conformer_convffn"


def rmsnorm(x, g):
    xf = x.astype(jnp.float32)
    y = xf * lax.rsqrt(jnp.mean(xf * xf, axis=-1, keepdims=True) + RMS_EPS)
    return (y * g.astype(jnp.float32)).astype(x.dtype)


def layernorm(x, g, b):
    xf = x.astype(jnp.float32)
    mu = jnp.mean(xf, axis=-1, keepdims=True)
    var = jnp.mean(jnp.square(xf - mu), axis=-1, keepdims=True)
    y = (xf - mu) * lax.rsqrt(var + LN_EPS)
    return (y * g.astype(jnp.float32) + b.astype(jnp.float32)).astype(x.dtype)


def causal_dwconv(x, w, b):
    k = w.shape[0]
    y = lax.conv_general_dilated(
        x, w[:, None, :].astype(x.dtype), window_strides=(1,), padding=[(k - 1, 0)],
        dimension_numbers=("NWC", "WIO", "NWC"), feature_group_count=x.shape[-1])
    return y + b.astype(x.dtype)


def multiscale_pool(u, w_pool, pool_scale):
    bsz, s, _ = u.shape
    ug = u.reshape(bsz, s, N_POOL_GROUPS, POOL_GROUP)
    c = jnp.cumsum(ug.astype(jnp.float32), axis=1)
    t = jnp.arange(s)
    means = []
    for g, w in enumerate(POOL_WINDOWS):
        cg = c[:, :, g]
        lag = jnp.pad(cg, ((0, 0), (w, 0), (0, 0)))[:, :s]
        cnt = jnp.minimum(t + 1, w).astype(jnp.float32)[None, :, None]
        means.append((cg - lag) / cnt)
    pooled = jnp.stack(means, axis=2).astype(u.dtype) - ug
    y = jnp.einsum("bsgc,gcd->bsgd", pooled, w_pool)
    return y.reshape(bsz, s, D_POOL) * pool_scale


def diff_attention(q, k, v, lam, lam_init, head_gain):
    bsz, s, h, _ = q.shape
    nb = s // Q_BLOCK
    slopes = jnp.exp2(-8.0 * (jnp.arange(h, dtype=jnp.float32) + 1.0) / h)
    kpos = jnp.arange(s)
    k1, k2 = k[..., :DIFF_QK_HALF], k[..., DIFF_QK_HALF:]
    qb = q.reshape(bsz, nb, Q_BLOCK, h, 2 * DIFF_QK_HALF).transpose(1, 0, 2, 3, 4)

    def block(args):
        qblk, i = args
        tpos = i * Q_BLOCK + jnp.arange(Q_BLOCK)
        allowed = (kpos // CHUNK)[None, :] <= (tpos // CHUNK)[:, None]
        dist = jnp.abs(tpos[:, None] - kpos[None, :]).astype(jnp.float32)
        bias = jnp.where(allowed[None], -slopes[:, None, None] * dist[None], -jnp.inf)

        def probs(qh, kh):
            sc = jnp.einsum("bqhd,bkhd->bhqk", qh, kh).astype(jnp.float32) * DIFF_SCALE + bias
            return jax.nn.softmax(sc, axis=-1)

        a = probs(qblk[..., :DIFF_QK_HALF], k1) - lam * probs(qblk[..., DIFF_QK_HALF:], k2)
        return jnp.einsum("bhqk,bkhd->bqhd", a.astype(v.dtype), v)

    o = lax.map(block, (qb, jnp.arange(nb)))
    o = o.transpose(1, 0, 2, 3, 4).reshape(bsz, s, h, DIFF_HEAD_DIM)
    of = o.astype(jnp.float32)
    of = of * lax.rsqrt(jnp.mean(of * of, axis=-1, keepdims=True) + RMS_EPS)
    of = of * head_gain.astype(jnp.float32) * (1.0 - lam_init)
    return of.astype(v.dtype).reshape(bsz, s, h * DIFF_HEAD_DIM)


def conformer_conv(c_in, dw_w, dw_b, ln_g, ln_b, pw_w, pw_b):
    a = c_in[..., :CONV_CH] * jax.nn.sigmoid(c_in[..., CONV_CH:])
    a = causal_dwconv(a, dw_w, dw_b)
    a = jax.nn.silu(layernorm(a, ln_g, ln_b))
    return a @ pw_w + pw_b


def setup_inputs(seed: int = 0) -> dict:
    key = jax.random.key(seed)
    ks = jax.random.split(key, 24)
    f32 = jnp.float32
    nrm = lambda k, shape, scale: jax.random.normal(k, shape, f32) * scale
    L = DEPTH
    return {
        "x": jax.random.normal(ks[0], (BATCH, SEQ, D_MODEL), f32),
        "norm_mix": 1.0 + nrm(ks[1], (L, D_MODEL), 0.05),
        "w_in": nrm(ks[2], (L, D_MODEL, D_IN), D_MODEL ** -0.5),
        "lam_q1": nrm(ks[3], (L, DIFF_QK_HALF), 0.1),
        "lam_k1": nrm(ks[4], (L, DIFF_QK_HALF), 0.1),
        "lam_q2": nrm(ks[5], (L, DIFF_QK_HALF), 0.1),
        "lam_k2": nrm(ks[6], (L, DIFF_QK_HALF), 0.1),
        "diff_head_gain": 1.0 + nrm(ks[7], (L, N_DIFF_HEADS, DIFF_HEAD_DIM), 0.05),
        "pool_w": nrm(ks[8], (L, N_POOL_GROUPS, POOL_GROUP, POOL_GROUP), POOL_GROUP ** -0.5),
        "pool_scale": 1.0 + nrm(ks[9], (L, D_POOL), 0.1),
        "conv_dw_w": nrm(ks[10], (L, CONV_K, CONV_CH), CONV_K ** -0.5),
        "conv_dw_b": nrm(ks[11], (L, CONV_CH), 0.02),
        "conv_ln_g": 1.0 + nrm(ks[12], (L, CONV_CH), 0.05),
        "conv_ln_b": nrm(ks[13], (L, CONV_CH), 0.02),
        "conv_pw_w": nrm(ks[14], (L, CONV_CH, CONV_CH), CONV_CH ** -0.5),
        "conv_pw_b": nrm(ks[15], (L, CONV_CH), 0.02),
        "w_out": nrm(ks[16], (L, D_MIX, D_MODEL), D_MIX ** -0.5),
        "norm_ffn": 1.0 + nrm(ks[17], (L, D_MODEL), 0.05),
        "w_ffn_in": nrm(ks[18], (L, D_MODEL, 2 * D_FF), D_MODEL ** -0.5),
        "ffn_conv_w": nrm(ks[19], (L, FFN_CONV_K, D_FF), FFN_CONV_K ** -0.5),
        "ffn_conv_b": nrm(ks[20], (L, D_FF), 0.02),
        "w_ffn_out": nrm(ks[21], (L, D_FF, D_MODEL), D_FF ** -0.5),
        "final_norm": 1.0 + nrm(ks[22], (D_MODEL,), 0.05),
    }


def reference(x, norm_mix, w_in, lam_q1, lam_k1, lam_q2, lam_k2, diff_head_gain,
              pool_w, pool_scale, conv_dw_w, conv_dw_b, conv_ln_g, conv_ln_b,
              conv_pw_w, conv_pw_b, w_out, norm_ffn, w_ffn_in, ffn_conv_w,
              ffn_conv_b, w_ffn_out, final_norm):
    bsz, s, _ = x.shape
    o_k = D_DIFF
    o_v = 2 * D_DIFF
    o_a = 3 * D_DIFF
    o_c = 3 * D_DIFF + D_POOL
    for l in range(DEPTH):
        h = rmsnorm(x, norm_mix[l])
        p = h @ w_in[l]
        q = p[..., :o_k].reshape(bsz, s, N_DIFF_HEADS, DIFF_HEAD_DIM)
        k = p[..., o_k:o_v].reshape(bsz, s, N_DIFF_HEADS, DIFF_HEAD_DIM)
        v = p[..., o_v:o_a].reshape(bsz, s, N_DIFF_HEADS, DIFF_HEAD_DIM)
        a_in = p[..., o_a:o_c]
        c_in = p[..., o_c:]

        lam_init = 0.8 - 0.6 * math.exp(-0.3 * l)
        lam = (jnp.exp(jnp.sum(lam_q1[l].astype(jnp.float32) * lam_k1[l].astype(jnp.float32)))
               - jnp.exp(jnp.sum(lam_q2[l].astype(jnp.float32) * lam_k2[l].astype(jnp.float32)))
               + lam_init)
        y_b = diff_attention(q, k, v, lam, lam_init, diff_head_gain[l])
        y_a = multiscale_pool(a_in, pool_w[l], pool_scale[l])
        y_c = conformer_conv(c_in, conv_dw_w[l], conv_dw_b[l], conv_ln_g[l],
                             conv_ln_b[l], conv_pw_w[l], conv_pw_b[l])
        x = x + jnp.concatenate([y_b, y_a, y_c], axis=-1) @ w_out[l]

        h = rmsnorm(x, norm_ffn[l])
        gu = h @ w_ffn_in[l]
        g = causal_dwconv(gu[..., :D_FF], ffn_conv_w[l], ffn_conv_b[l])
        x = x + (jax.nn.silu(g) * gu[..., D_FF:]) @ w_ffn_out[l]
    return rmsnorm(x, final_norm)
```

```python
import functools
import math

import jax
import jax.numpy as jnp
from jax import lax
from jax.experimental import pallas as pl
from jax.experimental.pallas import tpu as pltpu

F32 = jnp.float32
BF16 = jnp.bfloat16

D_MODEL = 2048
DEPTH = 4
CHUNK = 64
D_DIFF = 1024
HEAD_DIM = 128
QK_HALF = 64
N_HEADS = 8
QK_SCALE = 1.0 / math.sqrt(QK_HALF)
D_POOL = 512
POOL_WINDOWS = (2, 4, 8, 16)
POOL_GROUP = 128
CONV_CH = 512
CONV_K = 31
D_IN = 3 * D_DIFF + D_POOL + 2 * CONV_CH
D_MIX = D_DIFF + D_POOL + CONV_CH
D_FF = 5504
FFN_CONV_K = 3
RMS_EPS = 1e-6
LN_EPS = 1e-5

VMEM_LIMIT_BYTES = 56 * 1024 * 1024
SUBLANES = 8

IN_TM, IN_TN = 1024, 1152
ATT_T = 256
MIX_TS = 512
MIX_HALO = 32
OUT_TM, OUT_TN = 1024, 1024
FFN_TM, FFN_TF = 512, 512
D_FF_PAD = ((D_FF + FFN_TF - 1) // FFN_TF) * FFN_TF


def _params(*sem):
    return pltpu.CompilerParams(dimension_semantics=sem, vmem_limit_bytes=VMEM_LIMIT_BYTES)


def _rms(x, g):
    return x * lax.rsqrt(jnp.mean(x * x, axis=-1, keepdims=True) + RMS_EPS) * g


def _in_proj_kernel(x_ref, g_ref, w_ref, o_ref, h_ref):
    @pl.when(pl.program_id(1) == 0)
    def _():
        h_ref[...] = _rms(x_ref[...], g_ref[...]).astype(BF16)

    o_ref[...] = jnp.dot(h_ref[...], w_ref[...], preferred_element_type=F32).astype(o_ref.dtype)


def _in_proj(x2, g, w):
    m = x2.shape[0]
    tm = min(IN_TM, m)
    return pl.pallas_call(
        _in_proj_kernel,
        grid=(m // tm, D_IN // IN_TN),
        in_specs=[
            pl.BlockSpec((tm, D_MODEL), lambda i, j: (i, 0)),
            pl.BlockSpec((1, D_MODEL), lambda i, j: (0, 0)),
            pl.BlockSpec((D_MODEL, IN_TN), lambda i, j: (0, j)),
        ],
        out_specs=pl.BlockSpec((tm, IN_TN), lambda i, j: (i, j)),
        out_shape=jax.ShapeDtypeStruct((m, D_IN), BF16),
        scratch_shapes=[pltpu.VMEM((tm, D_MODEL), BF16)],
        compiler_params=_params("parallel", "arbitrary"),
        name="in_proj",
    )(x2, g, w)


def _attn_kernel(sc_ref, q_ref, k_ref, v_ref, pos_ref, gain_ref, o_ref,
                 kaug_ref, vt_ref, adj_ref, qa1_ref, qa2_ref,
                 m1_ref, l1_ref, acc1_ref, m2_ref, l2_ref, acc2_ref, *, n_blocks, out_scale):
    t = ATT_T
    h = pl.program_id(1)
    i = pl.program_id(2)
    lam = sc_ref[0]
    slope = sc_ref[1 + h]

    @pl.when(i == 0)
    def _():
        for c in range(n_blocks):
            rows = pl.ds(c * t, t)
            kaug_ref[c, :, 0:HEAD_DIM] = k_ref[rows, :]
            kaug_ref[c, :, HEAD_DIM:2 * HEAD_DIM] = pos_ref[rows, :]
            vt_ref[c] = v_ref[rows, :].astype(F32).T.astype(BF16)
        kk = lax.broadcasted_iota(jnp.int32, (t, t), 0)
        tt = lax.broadcasted_iota(jnp.int32, (t, t), 1)
        allowed = (kk // CHUNK) <= (tt // CHUNK)
        future = jnp.where(kk > tt, (2.0 * slope) * (tt - kk).astype(F32), 0.0)
        adj_ref[...] = jnp.where(allowed, future, -jnp.inf)

    qt = (q_ref[...].astype(F32) * QK_SCALE).T
    row = lax.broadcasted_iota(jnp.int32, (HEAD_DIM, t), 0)
    qa1_ref[0:HEAD_DIM, :] = jnp.where(row < QK_HALF, qt, 0.0).astype(BF16)
    qa2_ref[0:HEAD_DIM, :] = jnp.where(row >= QK_HALF, qt, 0.0).astype(BF16)
    t0 = (i * t).astype(F32)
    coef = jnp.where(row == 0, slope * 256.0,
                     jnp.where(row == 1, slope, jnp.where(row == 2, -slope * t0, 0.0))).astype(BF16)
    qa1_ref[HEAD_DIM:2 * HEAD_DIM, :] = coef
    qa2_ref[HEAD_DIM:2 * HEAD_DIM, :] = coef

    neg = jnp.full((1, t), -jnp.inf, F32)
    zero = jnp.zeros((1, t), F32)
    for m_ref, l_ref, acc_ref in ((m1_ref, l1_ref, acc1_ref), (m2_ref, l2_ref, acc2_ref)):
        m_ref[...] = neg
        l_ref[...] = zero
        acc_ref[...] = jnp.zeros((HEAD_DIM, t), F32)

    def update(s, vt, m_ref, l_ref, acc_ref):
        m_old = m_ref[...]
        m_new = jnp.maximum(m_old, jnp.max(s, axis=0, keepdims=True))
        alpha = jnp.exp(m_old - m_new)
        p = jnp.exp(s - m_new)
        l_ref[...] = alpha * l_ref[...] + jnp.sum(p, axis=0, keepdims=True)
        acc_ref[...] = alpha * acc_ref[...] + jnp.dot(vt, p.astype(BF16), preferred_element_type=F32)
        m_ref[...] = m_new

    def step(j, adj):
        kb = kaug_ref[j]
        vt = vt_ref[j]
        s1 = jnp.dot(kb, qa1_ref[...], preferred_element_type=F32)
        s2 = jnp.dot(kb, qa2_ref[...], preferred_element_type=F32)
        if adj is not None:
            s1 = s1 + adj
            s2 = s2 + adj
        update(s1, vt, m1_ref, l1_ref, acc1_ref)
        update(s2, vt, m2_ref, l2_ref, acc2_ref)

    def body(j, carry):
        step(j, None)
        return carry

    lax.fori_loop(0, i, body, 0)
    step(i, adj_ref[...])

    o = acc1_ref[...] / l1_ref[...] - lam * (acc2_ref[...] / l2_ref[...])
    o = o * lax.rsqrt(jnp.mean(o * o, axis=0, keepdims=True) + RMS_EPS)
    o_ref[...] = (o.T * gain_ref[...] * out_scale).astype(o_ref.dtype)


def _attention(p3, scalars, posfeat, gain, lam_init):
    b, s, _ = p3.shape
    t = ATT_T
    nb = s // t
    kern = functools.partial(_attn_kernel, n_blocks=nb, out_scale=1.0 - lam_init)
    hb = D_DIFF // HEAD_DIM
    return pl.pallas_call(
        kern,
        grid=(b, N_HEADS, nb),
        in_specs=[
            pl.BlockSpec(memory_space=pltpu.SMEM),
            pl.BlockSpec((None, t, HEAD_DIM), lambda b_, h, i: (b_, i, h)),
            pl.BlockSpec((None, s, HEAD_DIM), lambda b_, h, i: (b_, 0, hb + h)),
            pl.BlockSpec((None, s, HEAD_DIM), lambda b_, h, i: (b_, 0, 2 * hb + h)),
            pl.BlockSpec((s, HEAD_DIM), lambda b_, h, i: (0, 0)),
            pl.BlockSpec((None, 1, HEAD_DIM), lambda b_, h, i: (h, 0, 0)),
        ],
        out_specs=pl.BlockSpec((None, t, HEAD_DIM), lambda b_, h, i: (b_, i, h)),
        out_shape=jax.ShapeDtypeStruct((b, s, D_DIFF), BF16),
        scratch_shapes=[
            pltpu.VMEM((nb, t, 2 * HEAD_DIM), BF16),
            pltpu.VMEM((nb, HEAD_DIM, t), BF16),
            pltpu.VMEM((t, t), F32),
            pltpu.VMEM((2 * HEAD_DIM, t), BF16),
            pltpu.VMEM((2 * HEAD_DIM, t), BF16),
            pltpu.VMEM((1, t), F32), pltpu.VMEM((1, t), F32), pltpu.VMEM((HEAD_DIM, t), F32),
            pltpu.VMEM((1, t), F32), pltpu.VMEM((1, t), F32), pltpu.VMEM((HEAD_DIM, t), F32),
        ],
        compiler_params=_params("parallel", "parallel", "arbitrary"),
        name="diff_attention",
    )(scalars, p3, p3, p3, posfeat, gain)


def _mixer_kernel(a_ref, c1_ref, c2_ref, pw_ref, ps_ref, dww_ref, dwb_ref, lng_ref, lnb_ref,
                  cpw_ref, cpb_ref, o_ref, aext_ref, gext_ref, conv_ref):
    ts = MIX_TS
    hl = MIX_HALO
    si = pl.program_id(1)

    @pl.when(si == 0)
    def _():
        aext_ref[0:hl, :] = jnp.zeros((hl, D_POOL), F32)
        gext_ref[0:hl, :] = jnp.zeros((hl, CONV_CH), F32)

    @pl.when(si != 0)
    def _():
        aext_ref[0:hl, :] = aext_ref[ts:ts + hl, :]
        gext_ref[0:hl, :] = gext_ref[ts:ts + hl, :]

    a = a_ref[...].astype(F32)
    aext_ref[hl:hl + ts, :] = a
    c1 = c1_ref[...].astype(F32)
    c2 = c2_ref[...].astype(F32)
    gext_ref[hl:hl + ts, :] = c1 * jax.nn.sigmoid(c2)

    tpos = si * ts + lax.broadcasted_iota(jnp.int32, (ts, 1), 0)
    for g, w in enumerate(POOL_WINDOWS):
        cols = slice(g * POOL_GROUP, (g + 1) * POOL_GROUP)
        acc = aext_ref[hl:hl + ts, cols]
        for d in range(1, w):
            acc = acc + aext_ref[hl - d:hl - d + ts, cols]
        cnt = jnp.minimum(tpos + 1, w).astype(F32)
        pooled = acc / cnt - aext_ref[hl:hl + ts, cols]
        y = jnp.dot(pooled.astype(BF16), pw_ref[g], preferred_element_type=F32)
        o_ref[:, cols] = (y * ps_ref[:, cols]).astype(o_ref.dtype)

    rc = 64
    for r0 in range(0, ts, rc):
        for c0 in range(0, CONV_CH, 128):
            cols = slice(c0, c0 + 128)
            acc = jnp.zeros((rc, 128), F32) + dwb_ref[:, cols]
            for j in range(CONV_K):
                off = hl + r0 - (CONV_K - 1) + j
                acc = acc + dww_ref[j:j + 1, cols] * gext_ref[off:off + rc, cols]
            conv_ref[r0:r0 + rc, cols] = acc

    u = conv_ref[...]
    mu = jnp.mean(u, axis=-1, keepdims=True)
    var = jnp.mean(jnp.square(u - mu), axis=-1, keepdims=True)
    z = (u - mu) * lax.rsqrt(var + LN_EPS) * lng_ref[...] + lnb_ref[...]
    z = z * jax.nn.sigmoid(z)
    y = jnp.dot(z.astype(BF16), cpw_ref[...], preferred_element_type=F32) + cpb_ref[...]
    o_ref[:, D_POOL:D_POOL + CONV_CH] = y.astype(o_ref.dtype)


def _mixer(p3, pool_w, pool_scale, dw_w, dw_b, ln_g, ln_b, pw_w, pw_b):
    b, s, _ = p3.shape
    ts = MIX_TS
    a_blk = (3 * D_DIFF) // D_POOL
    full = lambda shape: pl.BlockSpec(shape, lambda b_, i: (0,) * len(shape))
    return pl.pallas_call(
        _mixer_kernel,
        grid=(b, s // ts),
        in_specs=[
            pl.BlockSpec((None, ts, D_POOL), lambda b_, i: (b_, i, a_blk)),
            pl.BlockSpec((None, ts, CONV_CH), lambda b_, i: (b_, i, a_blk + 1)),
            pl.BlockSpec((None, ts, CONV_CH), lambda b_, i: (b_, i, a_blk + 2)),
            full((len(POOL_WINDOWS), POOL_GROUP, POOL_GROUP)),
            full((1, D_POOL)),
            full((CONV_K, CONV_CH)),
            full((1, CONV_CH)),
            full((1, CONV_CH)),
            full((1, CONV_CH)),
            full((CONV_CH, CONV_CH)),
            full((1, CONV_CH)),
        ],
        out_specs=pl.BlockSpec((None, ts, D_POOL + CONV_CH), lambda b_, i: (b_, i, 0)),
        out_shape=jax.ShapeDtypeStruct((b, s, D_POOL + CONV_CH), BF16),
        scratch_shapes=[
            pltpu.VMEM((MIX_HALO + ts, D_POOL), F32),
            pltpu.VMEM((MIX_HALO + ts, CONV_CH), F32),
            pltpu.VMEM((ts, CONV_CH), F32),
        ],
        compiler_params=_params("parallel", "arbitrary"),
        name="pool_conformer",
    )(p3, p3, p3, pool_w, pool_scale, dw_w, dw_b, ln_g, ln_b, pw_w, pw_b)


def _out_proj_kernel(x_ref, yb_ref, yac_ref, wb_ref, wac_ref, o_ref):
    acc = jnp.dot(yb_ref[...], wb_ref[...], preferred_element_type=F32)
    acc = acc + jnp.dot(yac_ref[...], wac_ref[...], preferred_element_type=F32)
    o_ref[...] = x_ref[...] + acc


def _out_proj(x2, yb, yac, w):
    m = x2.shape[0]
    tm = min(OUT_TM, m)
    tn = OUT_TN
    kb = D_DIFF
    return pl.pallas_call(
        _out_proj_kernel,
        grid=(m // tm, D_MODEL // tn),
        in_specs=[
            pl.BlockSpec((tm, tn), lambda i, j: (i, j)),
            pl.BlockSpec((tm, kb), lambda i, j: (i, 0)),
            pl.BlockSpec((tm, D_MIX - kb), lambda i, j: (i, 0)),
            pl.BlockSpec((kb, tn), lambda i, j: (0, j)),
            pl.BlockSpec((D_MIX - kb, tn), lambda i, j: (1, j)),
        ],
        out_specs=pl.BlockSpec((tm, tn), lambda i, j: (i, j)),
        out_shape=jax.ShapeDtypeStruct((m, D_MODEL), F32),
        compiler_params=_params("parallel", "arbitrary"),
        name="out_proj",
    )(x2, yb, yac, w, w)


def _ffn_kernel(x_ref, g_ref, wg_ref, wu_ref, cw_ref, cb_ref, wo_ref, o_ref,
                h_ref, gext_ref, halo_ref, *, tiles_per_seq):
    tm = FFN_TM
    mi = pl.program_id(0)
    f = pl.program_id(1)

    @pl.when(f == 0)
    def _():
        x = x_ref[...]
        h_ref[...] = _rms(x, g_ref[...]).astype(BF16)
        o_ref[...] = x

    h = h_ref[...]
    gate = jnp.dot(h, wg_ref[...], preferred_element_type=F32)
    up = jnp.dot(h, wu_ref[...], preferred_element_type=F32)

    first = (mi % tiles_per_seq) == 0

    @pl.when(first)
    def _():
        gext_ref[0:SUBLANES, :] = jnp.zeros((SUBLANES, FFN_TF), F32)

    @pl.when(jnp.logical_not(first))
    def _():
        gext_ref[0:SUBLANES, :] = halo_ref[f]

    gext_ref[SUBLANES:SUBLANES + tm, :] = gate
    halo_ref[f] = gate[tm - SUBLANES:, :]
    g = (cw_ref[0:1, :] * gext_ref[SUBLANES - 2:SUBLANES - 2 + tm, :]
         + cw_ref[1:2, :] * gext_ref[SUBLANES - 1:SUBLANES - 1 + tm, :]
         + cw_ref[2:3, :] * gate + cb_ref[...])
    act = (g * jax.nn.sigmoid(g)) * up
    o_ref[...] += jnp.dot(act.astype(BF16), wo_ref[...], preferred_element_type=F32)


def _ffn(x2, g, wg, wu, cw, cb, wo, seq):
    m = x2.shape[0]
    tm, tf = FFN_TM, FFN_TF
    nf = D_FF_PAD // tf
    kern = functools.partial(_ffn_kernel, tiles_per_seq=seq // tm)
    return pl.pallas_call(
        kern,
        grid=(m // tm, nf),
        in_specs=[
            pl.BlockSpec((tm, D_MODEL), lambda i, j: (i, 0)),
            pl.BlockSpec((1, D_MODEL), lambda i, j: (0, 0)),
            pl.BlockSpec((D_MODEL, tf), lambda i, j: (0, j)),
            pl.BlockSpec((D_MODEL, tf), lambda i, j: (0, j)),
            pl.BlockSpec((FFN_CONV_K, tf), lambda i, j: (0, j)),
            pl.BlockSpec((1, tf), lambda i, j: (0, j)),
            pl.BlockSpec((tf, D_MODEL), lambda i, j: (j, 0)),
        ],
        out_specs=pl.BlockSpec((tm, D_MODEL), lambda i, j: (i, 0)),
        out_shape=jax.ShapeDtypeStruct((m, D_MODEL), F32),
        scratch_shapes=[
            pltpu.VMEM((tm, D_MODEL), BF16),
            pltpu.VMEM((SUBLANES + tm, tf), F32),
            pltpu.VMEM((nf, SUBLANES, tf), F32),
        ],
        compiler_params=_params("arbitrary", "arbitrary"),
        name="convglu_ffn",
    )(x2, g, wg, wu, cw, cb, wo)


def _final_norm_kernel(x_ref, g_ref, o_ref):
    o_ref[...] = _rms(x_ref[...], g_ref[...])


def _final_norm(x2, g):
    m = x2.shape[0]
    tm = min(1024, m)
    return pl.pallas_call(
        _final_norm_kernel,
        grid=(m // tm,),
        in_specs=[pl.BlockSpec((tm, D_MODEL), lambda i: (i, 0)),
                  pl.BlockSpec((1, D_MODEL), lambda i: (0, 0))],
        out_specs=pl.BlockSpec((tm, D_MODEL), lambda i: (i, 0)),
        out_shape=jax.ShapeDtypeStruct((m, D_MODEL), F32),
        compiler_params=_params("parallel"),
        name="final_norm",
    )(x2, g)


def _pad_ff(a, axis):
    pad = [(0, 0)] * a.ndim
    pad[axis] = (0, D_FF_PAD - D_FF)
    return jnp.pad(a, pad)


def kernel(x, norm_mix, w_in, lam_q1, lam_k1, lam_q2, lam_k2, diff_head_gain, pool_w, pool_scale,
           conv_dw_w, conv_dw_b, conv_ln_g, conv_ln_b, conv_pw_w, conv_pw_b, w_out, norm_ffn,
           w_ffn_in, ffn_conv_w, ffn_conv_b, w_ffn_out, final_norm):
    bsz, seq, _ = x.shape
    m = bsz * seq
    assert seq % MIX_TS == 0 and seq % ATT_T == 0 and seq % FFN_TM == 0 and seq <= 256 * 256

    slopes = jnp.asarray([2.0 ** (-8.0 * (h + 1.0) / N_HEADS) for h in range(N_HEADS)], F32)
    kpos = jnp.arange(seq, dtype=jnp.int32)
    posfeat = jnp.zeros((seq, HEAD_DIM), F32)
    posfeat = posfeat.at[:, 0].set((kpos // 256).astype(F32))
    posfeat = posfeat.at[:, 1].set((kpos % 256).astype(F32))
    posfeat = posfeat.at[:, 2].set(1.0).astype(BF16)

    x2 = x.reshape(m, D_MODEL)
    for l in range(DEPTH):
        lam_init = 0.8 - 0.6 * math.exp(-0.3 * l)
        lam = (jnp.exp(jnp.sum(lam_q1[l] * lam_k1[l])) - jnp.exp(jnp.sum(lam_q2[l] * lam_k2[l])) + lam_init)
        scalars = jnp.concatenate([lam.reshape(1), slopes]).astype(F32)

        p = _in_proj(x2, norm_mix[l].reshape(1, D_MODEL), w_in[l].astype(BF16))
        p3 = p.reshape(bsz, seq, D_IN)
        y_b = _attention(p3, scalars, posfeat, diff_head_gain[l].reshape(N_HEADS, 1, HEAD_DIM), lam_init)
        y_ac = _mixer(p3, pool_w[l].astype(BF16), pool_scale[l].reshape(1, D_POOL), conv_dw_w[l],
                      conv_dw_b[l].reshape(1, CONV_CH), conv_ln_g[l].reshape(1, CONV_CH),
                      conv_ln_b[l].reshape(1, CONV_CH), conv_pw_w[l].astype(BF16),
                      conv_pw_b[l].reshape(1, CONV_CH))
        x2 = _out_proj(x2, y_b.reshape(m, D_DIFF), y_ac.reshape(m, D_POOL + CONV_CH), w_out[l].astype(BF16))

        wg = _pad_ff(w_ffn_in[l][:, :D_FF].astype(BF16), 1)
        wu = _pad_ff(w_ffn_in[l][:, D_FF:].astype(BF16), 1)
        cw = _pad_ff(ffn_conv_w[l], 1)
        cb = _pad_ff(ffn_conv_b[l].reshape(1, D_FF), 1)
        wo = _pad_ff(w_ffn_out[l].astype(BF16), 0)
        x2 = _ffn(x2, norm_ffn[l].reshape(1, D_MODEL), wg, wu, cw, cb, wo, seq)

    out = _final_norm(x2, final_norm.reshape(1, D_MODEL))
    return out.reshape(bsz, seq, D_MODEL)
```

```python
import functools
import math

import jax
import jax.numpy as jnp
import numpy as np
from jax import lax
from jax.experimental import pallas as pl
from jax.experimental.pallas import tpu as pltpu

F32 = jnp.float32
BF16 = jnp.bfloat16

D_MODEL = 2048
DEPTH = 4
CHUNK = 64
D_DIFF = 1024
HEAD_DIM = 128
QK_HALF = 64
N_HEADS = 8
QK_SCALE = 1.0 / math.sqrt(QK_HALF)
LOG2E = 1.4426950408889634
D_POOL = 512
POOL_WINDOWS = (2, 4, 8, 16)
POOL_GROUP = 128
CONV_CH = 512
CONV_K = 31
D_IN = 3 * D_DIFF + D_POOL + 2 * CONV_CH
D_MIX = D_DIFF + D_POOL + CONV_CH
D_FF = 5504
FFN_CONV_K = 3
RMS_EPS = 1e-6
LN_EPS = 1e-5

VMEM_LIMIT_BYTES = 56 * 1024 * 1024
SUBLANES = 8

IN_TM, IN_TN = 1024, 1152
ATT_T = 512
MIX_TS = 512
MIX_HALO = 32
OUT_TM, OUT_TN = 1024, 1024
FFN_TM, FFN_TF = 512, 512
D_FF_PAD = ((D_FF + FFN_TF - 1) // FFN_TF) * FFN_TF


def _params(*sem):
    return pltpu.CompilerParams(dimension_semantics=sem, vmem_limit_bytes=VMEM_LIMIT_BYTES)


def _rms(x, g):
    return x * lax.rsqrt(jnp.mean(x * x, axis=-1, keepdims=True) + RMS_EPS) * g


def _in_proj_kernel(x_ref, g_ref, w_ref, o_ref, h_ref):
    @pl.when(pl.program_id(1) == 0)
    def _():
        h_ref[...] = _rms(x_ref[...], g_ref[...]).astype(BF16)

    o_ref[...] = jnp.dot(h_ref[...], w_ref[...], preferred_element_type=F32).astype(o_ref.dtype)


def _in_proj(x2, g, w):
    m = x2.shape[0]
    tm = min(IN_TM, m)
    return pl.pallas_call(
        _in_proj_kernel,
        grid=(m // tm, D_IN // IN_TN),
        in_specs=[
            pl.BlockSpec((tm, D_MODEL), lambda i, j: (i, 0)),
            pl.BlockSpec((1, D_MODEL), lambda i, j: (0, 0)),
            pl.BlockSpec((D_MODEL, IN_TN), lambda i, j: (0, j)),
        ],
        out_specs=pl.BlockSpec((tm, IN_TN), lambda i, j: (i, j)),
        out_shape=jax.ShapeDtypeStruct((m, D_IN), BF16),
        scratch_shapes=[pltpu.VMEM((tm, D_MODEL), BF16)],
        compiler_params=_params("parallel", "arbitrary"),
        name="in_proj",
    )(x2, g, w)


def _attn_kernel(sc_ref, q_ref, k_ref, v_ref, pos_ref, gain_ref, o_ref,
                 kaug_ref, vt_ref, adj_ref, qa1_ref, qa2_ref,
                 m1_ref, l1_ref, acc1_ref, m2_ref, l2_ref, acc2_ref, *, n_blocks, out_scale):
    t = ATT_T
    h = pl.program_id(1)
    i = pl.program_id(2)
    lam = sc_ref[0]
    c_hi = sc_ref[1 + h]
    c_lo = sc_ref[1 + N_HEADS + h]
    row = lax.broadcasted_iota(jnp.int32, (HEAD_DIM, t), 0)

    @pl.when(i == 0)
    def _():
        for c in range(n_blocks):
            rows = pl.ds(c * t, t)
            kaug_ref[c, :, 0:HEAD_DIM] = k_ref[rows, :]
            kaug_ref[c, :, HEAD_DIM:2 * HEAD_DIM] = pos_ref[rows, :]
            vt_ref[c] = v_ref[rows, :].astype(F32).T.astype(BF16)
        kk = lax.broadcasted_iota(jnp.int32, (t, t), 0)
        tt = lax.broadcasted_iota(jnp.int32, (t, t), 1)
        allowed = (kk // CHUNK) <= (tt // CHUNK)
        future = jnp.where(kk > tt, (2.0 * (c_hi + c_lo)) * (tt - kk).astype(F32), 0.0)
        adj_ref[...] = jnp.where(allowed, future, -jnp.inf)
        coef = jnp.where(row == 0, c_hi * 256.0,
                         jnp.where(row == 1, c_hi,
                                   jnp.where(row == 2, c_lo * 256.0,
                                             jnp.where(row == 3, c_lo, 0.0)))).astype(BF16)
        qa1_ref[HEAD_DIM:2 * HEAD_DIM, :] = coef
        qa2_ref[HEAD_DIM:2 * HEAD_DIM, :] = coef

    qt = (q_ref[...].astype(F32) * (QK_SCALE * LOG2E)).T
    qa1_ref[0:HEAD_DIM, :] = jnp.where(row < QK_HALF, qt, 0.0).astype(BF16)
    qa2_ref[0:HEAD_DIM, :] = jnp.where(row >= QK_HALF, qt, 0.0).astype(BF16)

    for m_ref, l_ref, acc_ref in ((m1_ref, l1_ref, acc1_ref), (m2_ref, l2_ref, acc2_ref)):
        m_ref[...] = jnp.full((1, t), -jnp.inf, F32)
        l_ref[...] = jnp.zeros((1, t), F32)
        acc_ref[...] = jnp.zeros((HEAD_DIM, t), F32)

    def update(s, vt, m_ref, l_ref, acc_ref):
        m_old = m_ref[...]
        m_new = jnp.maximum(m_old, jnp.max(s, axis=0, keepdims=True))
        alpha = jnp.exp2(m_old - m_new)
        p = jnp.exp2(s - m_new)
        l_ref[...] = alpha * l_ref[...] + jnp.sum(p, axis=0, keepdims=True)
        acc_ref[...] = alpha * acc_ref[...] + jnp.dot(vt, p.astype(BF16), preferred_element_type=F32)
        m_ref[...] = m_new

    def step(j, adj):
        kb = kaug_ref[j]
        vt = vt_ref[j]
        s1 = jnp.dot(kb, qa1_ref[...], preferred_element_type=F32)
        s2 = jnp.dot(kb, qa2_ref[...], preferred_element_type=F32)
        if adj is not None:
            s1 = s1 + adj
            s2 = s2 + adj
        update(s1, vt, m1_ref, l1_ref, acc1_ref)
        update(s2, vt, m2_ref, l2_ref, acc2_ref)

    def body(j, carry):
        step(j, None)
        return carry

    lax.fori_loop(0, i, body, 0)
    step(i, adj_ref[...])

    o = acc1_ref[...] / l1_ref[...] - lam * (acc2_ref[...] / l2_ref[...])
    o = o * lax.rsqrt(jnp.mean(o * o, axis=0, keepdims=True) + RMS_EPS)
    o_ref[...] = (o.T * gain_ref[...] * out_scale).astype(o_ref.dtype)


def _attention(p3, scalars, posfeat, gain, lam_init):
    b, s, _ = p3.shape
    t = ATT_T
    nb = s // t
    kern = functools.partial(_attn_kernel, n_blocks=nb, out_scale=1.0 - lam_init)
    hb = D_DIFF // HEAD_DIM
    return pl.pallas_call(
        kern,
        grid=(b, N_HEADS, nb),
        in_specs=[
            pl.BlockSpec(memory_space=pltpu.SMEM),
            pl.BlockSpec((None, t, HEAD_DIM), lambda b_, h, i: (b_, i, h)),
            pl.BlockSpec((None, s, HEAD_DIM), lambda b_, h, i: (b_, 0, hb + h)),
            pl.BlockSpec((None, s, HEAD_DIM), lambda b_, h, i: (b_, 0, 2 * hb + h)),
            pl.BlockSpec((s, HEAD_DIM), lambda b_, h, i: (0, 0)),
            pl.BlockSpec((None, 1, HEAD_DIM), lambda b_, h, i: (h, 0, 0)),
        ],
        out_specs=pl.BlockSpec((None, t, HEAD_DIM), lambda b_, h, i: (b_, i, h)),
        out_shape=jax.ShapeDtypeStruct((b, s, D_DIFF), BF16),
        scratch_shapes=[
            pltpu.VMEM((nb, t, 2 * HEAD_DIM), BF16),
            pltpu.VMEM((nb, HEAD_DIM, t), BF16),
            pltpu.VMEM((t, t), F32),
            pltpu.VMEM((2 * HEAD_DIM, t), BF16),
            pltpu.VMEM((2 * HEAD_DIM, t), BF16),
            pltpu.VMEM((1, t), F32), pltpu.VMEM((1, t), F32), pltpu.VMEM((HEAD_DIM, t), F32),
            pltpu.VMEM((1, t), F32), pltpu.VMEM((1, t), F32), pltpu.VMEM((HEAD_DIM, t), F32),
        ],
        compiler_params=_params("parallel", "parallel", "arbitrary"),
        name="diff_attention",
    )(scalars, p3, p3, p3, posfeat, gain)


def _mixer_kernel(a_ref, c1_ref, c2_ref, pw_ref, ps_ref, dww_ref, dwb_ref, lng_ref, lnb_ref,
                  cpw_ref, cpb_ref, o_ref, aext_ref, gext_ref, conv_ref):
    ts = MIX_TS
    hl = MIX_HALO
    si = pl.program_id(1)

    @pl.when(si == 0)
    def _():
        aext_ref[0:hl, :] = jnp.zeros((hl, D_POOL), F32)
        gext_ref[0:hl, :] = jnp.zeros((hl, CONV_CH), F32)

    @pl.when(si != 0)
    def _():
        aext_ref[0:hl, :] = aext_ref[ts:ts + hl, :]
        gext_ref[0:hl, :] = gext_ref[ts:ts + hl, :]

    a = a_ref[...].astype(F32)
    aext_ref[hl:hl + ts, :] = a
    c1 = c1_ref[...].astype(F32)
    c2 = c2_ref[...].astype(F32)
    gext_ref[hl:hl + ts, :] = c1 * jax.nn.sigmoid(c2)

    tpos = si * ts + lax.broadcasted_iota(jnp.int32, (ts, 1), 0)
    for g, w in enumerate(POOL_WINDOWS):
        cols = slice(g * POOL_GROUP, (g + 1) * POOL_GROUP)
        acc = aext_ref[hl:hl + ts, cols]
        for d in range(1, w):
            acc = acc + aext_ref[hl - d:hl - d + ts, cols]
        cnt = jnp.minimum(tpos + 1, w).astype(F32)
        pooled = acc / cnt - aext_ref[hl:hl + ts, cols]
        y = jnp.dot(pooled.astype(BF16), pw_ref[g], preferred_element_type=F32)
        o_ref[:, cols] = (y * ps_ref[:, cols]).astype(o_ref.dtype)

    rc = 64
    for r0 in range(0, ts, rc):
        for c0 in range(0, CONV_CH, 128):
            cols = slice(c0, c0 + 128)
            acc = jnp.zeros((rc, 128), F32) + dwb_ref[:, cols]
            for j in range(CONV_K):
                off = hl + r0 - (CONV_K - 1) + j
                acc = acc + dww_ref[j:j + 1, cols] * gext_ref[off:off + rc, cols]
            conv_ref[r0:r0 + rc, cols] = acc

    u = conv_ref[...]
    mu = jnp.mean(u, axis=-1, keepdims=True)
    var = jnp.mean(jnp.square(u - mu), axis=-1, keepdims=True)
    z = (u - mu) * lax.rsqrt(var + LN_EPS) * lng_ref[...] + lnb_ref[...]
    z = z * jax.nn.sigmoid(z)
    y = jnp.dot(z.astype(BF16), cpw_ref[...], preferred_element_type=F32) + cpb_ref[...]
    o_ref[:, D_POOL:D_POOL + CONV_CH] = y.astype(o_ref.dtype)


def _mixer(p3, pool_w, pool_scale, dw_w, dw_b, ln_g, ln_b, pw_w, pw_b):
    b, s, _ = p3.shape
    ts = MIX_TS
    a_blk = (3 * D_DIFF) // D_POOL
    full = lambda shape: pl.BlockSpec(shape, lambda b_, i: (0,) * len(shape))
    return pl.pallas_call(
        _mixer_kernel,
        grid=(b, s // ts),
        in_specs=[
            pl.BlockSpec((None, ts, D_POOL), lambda b_, i: (b_, i, a_blk)),
            pl.BlockSpec((None, ts, CONV_CH), lambda b_, i: (b_, i, a_blk + 1)),
            pl.BlockSpec((None, ts, CONV_CH), lambda b_, i: (b_, i, a_blk + 2)),
            full((len(POOL_WINDOWS), POOL_GROUP, POOL_GROUP)),
            full((1, D_POOL)),
            full((CONV_K, CONV_CH)),
            full((1, CONV_CH)),
            full((1, CONV_CH)),
            full((1, CONV_CH)),
            full((CONV_CH, CONV_CH)),
            full((1, CONV_CH)),
        ],
        out_specs=pl.BlockSpec((None, ts, D_POOL + CONV_CH), lambda b_, i: (b_, i, 0)),
        out_shape=jax.ShapeDtypeStruct((b, s, D_POOL + CONV_CH), BF16),
        scratch_shapes=[
            pltpu.VMEM((MIX_HALO + ts, D_POOL), F32),
            pltpu.VMEM((MIX_HALO + ts, CONV_CH), F32),
            pltpu.VMEM((ts, CONV_CH), F32),
        ],
        compiler_params=_params("parallel", "arbitrary"),
        name="pool_conformer",
    )(p3, p3, p3, pool_w, pool_scale, dw_w, dw_b, ln_g, ln_b, pw_w, pw_b)


def _out_proj_kernel(x_ref, yb_ref, yac_ref, wb_ref, wac_ref, o_ref):
    acc = jnp.dot(yb_ref[...], wb_ref[...], preferred_element_type=F32)
    acc = acc + jnp.dot(yac_ref[...], wac_ref[...], preferred_element_type=F32)
    o_ref[...] = x_ref[...] + acc


def _out_proj(x2, yb, yac, w):
    m = x2.shape[0]
    tm = min(OUT_TM, m)
    tn = OUT_TN
    kb = D_DIFF
    return pl.pallas_call(
        _out_proj_kernel,
        grid=(m // tm, D_MODEL // tn),
        in_specs=[
            pl.BlockSpec((tm, tn), lambda i, j: (i, j)),
            pl.BlockSpec((tm, kb), lambda i, j: (i, 0)),
            pl.BlockSpec((tm, D_MIX - kb), lambda i, j: (i, 0)),
            pl.BlockSpec((kb, tn), lambda i, j: (0, j)),
            pl.BlockSpec((D_MIX - kb, tn), lambda i, j: (1, j)),
        ],
        out_specs=pl.BlockSpec((tm, tn), lambda i, j: (i, j)),
        out_shape=jax.ShapeDtypeStruct((m, D_MODEL), F32),
        compiler_params=_params("parallel", "arbitrary"),
        name="out_proj",
    )(x2, yb, yac, w, w)


def _ffn_kernel(x_ref, g_ref, wg_ref, wu_ref, cw_ref, cb_ref, wo_ref, o_ref,
                h_ref, gext_ref, halo_ref, *, tiles_per_seq):
    tm = FFN_TM
    mi = pl.program_id(0)
    f = pl.program_id(1)

    @pl.when(f == 0)
    def _():
        x = x_ref[...]
        h_ref[...] = _rms(x, g_ref[...]).astype(BF16)
        o_ref[...] = x

    h = h_ref[...]
    gate = jnp.dot(h, wg_ref[...], preferred_element_type=F32)
    up = jnp.dot(h, wu_ref[...], preferred_element_type=F32)

    first = (mi % tiles_per_seq) == 0

    @pl.when(first)
    def _():
        gext_ref[0:SUBLANES, :] = jnp.zeros((SUBLANES, FFN_TF), F32)

    @pl.when(jnp.logical_not(first))
    def _():
        gext_ref[0:SUBLANES, :] = halo_ref[f]

    gext_ref[SUBLANES:SUBLANES + tm, :] = gate
    halo_ref[f] = gate[tm - SUBLANES:, :]
    g = (cw_ref[0:1, :] * gext_ref[SUBLANES - 2:SUBLANES - 2 + tm, :]
         + cw_ref[1:2, :] * gext_ref[SUBLANES - 1:SUBLANES - 1 + tm, :]
         + cw_ref[2:3, :] * gate + cb_ref[...])
    act = (g * jax.nn.sigmoid(g)) * up
    o_ref[...] += jnp.dot(act.astype(BF16), wo_ref[...], preferred_element_type=F32)


def _ffn(x2, g, wg, wu, cw, cb, wo, seq):
    m = x2.shape[0]
    tm, tf = FFN_TM, FFN_TF
    nf = D_FF_PAD // tf
    kern = functools.partial(_ffn_kernel, tiles_per_seq=seq // tm)
    return pl.pallas_call(
        kern,
        grid=(m // tm, nf),
        in_specs=[
            pl.BlockSpec((tm, D_MODEL), lambda i, j: (i, 0)),
            pl.BlockSpec((1, D_MODEL), lambda i, j: (0, 0)),
            pl.BlockSpec((D_MODEL, tf), lambda i, j: (0, j)),
            pl.BlockSpec((D_MODEL, tf), lambda i, j: (0, j)),
            pl.BlockSpec((FFN_CONV_K, tf), lambda i, j: (0, j)),
            pl.BlockSpec((1, tf), lambda i, j: (0, j)),
            pl.BlockSpec((tf, D_MODEL), lambda i, j: (j, 0)),
        ],
        out_specs=pl.BlockSpec((tm, D_MODEL), lambda i, j: (i, 0)),
        out_shape=jax.ShapeDtypeStruct((m, D_MODEL), F32),
        scratch_shapes=[
            pltpu.VMEM((tm, D_MODEL), BF16),
            pltpu.VMEM((SUBLANES + tm, tf), F32),
            pltpu.VMEM((nf, SUBLANES, tf), F32),
        ],
        compiler_params=_params("arbitrary", "arbitrary"),
        name="convglu_ffn",
    )(x2, g, wg, wu, cw, cb, wo)


def _final_norm_kernel(x_ref, g_ref, o_ref):
    o_ref[...] = _rms(x_ref[...], g_ref[...])


def _final_norm(x2, g):
    m = x2.shape[0]
    tm = min(1024, m)
    return pl.pallas_call(
        _final_norm_kernel,
        grid=(m // tm,),
        in_specs=[pl.BlockSpec((tm, D_MODEL), lambda i: (i, 0)),
                  pl.BlockSpec((1, D_MODEL), lambda i: (0, 0))],
        out_specs=pl.BlockSpec((tm, D_MODEL), lambda i: (i, 0)),
        out_shape=jax.ShapeDtypeStruct((m, D_MODEL), F32),
        compiler_params=_params("parallel"),
        name="final_norm",
    )(x2, g)


def _pad_ff(a, axis):
    pad = [(0, 0)] * a.ndim
    pad[axis] = (0, D_FF_PAD - D_FF)
    return jnp.pad(a, pad)


def kernel(x, norm_mix, w_in, lam_q1, lam_k1, lam_q2, lam_k2, diff_head_gain, pool_w, pool_scale,
           conv_dw_w, conv_dw_b, conv_ln_g, conv_ln_b, conv_pw_w, conv_pw_b, w_out, norm_ffn,
           w_ffn_in, ffn_conv_w, ffn_conv_b, w_ffn_out, final_norm):
    bsz, seq, _ = x.shape
    m = bsz * seq
    assert seq % MIX_TS == 0 and seq % ATT_T == 0 and seq % FFN_TM == 0 and seq <= 256 * 256

    coef = np.asarray([2.0 ** (-8.0 * (h + 1.0) / N_HEADS) * LOG2E for h in range(N_HEADS)], np.float64)
    coef_hi = coef.astype(BF16).astype(np.float64)
    coef_lo = (coef - coef_hi).astype(BF16).astype(np.float64)
    coefs = jnp.asarray(np.concatenate([coef_hi, coef_lo]), F32)
    kpos = np.arange(seq)
    posfeat = np.zeros((seq, HEAD_DIM), np.float32)
    posfeat[:, 0] = posfeat[:, 2] = kpos // 256
    posfeat[:, 1] = posfeat[:, 3] = kpos % 256
    posfeat = jnp.asarray(posfeat, BF16)

    x2 = x.reshape(m, D_MODEL)
    for l in range(DEPTH):
        lam_init = 0.8 - 0.6 * math.exp(-0.3 * l)
        lam = (jnp.exp(jnp.sum(lam_q1[l] * lam_k1[l])) - jnp.exp(jnp.sum(lam_q2[l] * lam_k2[l])) + lam_init)
        scalars = jnp.concatenate([lam.reshape(1).astype(F32), coefs])

        p = _in_proj(x2, norm_mix[l].reshape(1, D_MODEL), w_in[l].astype(BF16))
        p3 = p.reshape(bsz, seq, D_IN)
        y_b = _attention(p3, scalars, posfeat, diff_head_gain[l].reshape(N_HEADS, 1, HEAD_DIM), lam_init)
        y_ac = _mixer(p3, pool_w[l].astype(BF16), pool_scale[l].reshape(1, D_POOL), conv_dw_w[l],
                      conv_dw_b[l].reshape(1, CONV_CH), conv_ln_g[l].reshape(1, CONV_CH),
                      conv_ln_b[l].reshape(1, CONV_CH), conv_pw_w[l].astype(BF16),
                      conv_pw_b[l].reshape(1, CONV_CH))
        x2 = _out_proj(x2, y_b.reshape(m, D_DIFF), y_ac.reshape(m, D_POOL + CONV_CH), w_out[l].astype(BF16))

        wg = _pad_ff(w_ffn_in[l][:, :D_FF].astype(BF16), 1)
        wu = _pad_ff(w_ffn_in[l][:, D_FF:].astype(BF16), 1)
        cw = _pad_ff(ffn_conv_w[l], 1)
        cb = _pad_ff(ffn_conv_b[l].reshape(1, D_FF), 1)
        wo = _pad_ff(w_ffn_out[l].astype(BF16), 0)
        x2 = _ffn(x2, norm_ffn[l].reshape(1, D_MODEL), wg, wu, cw, cb, wo, seq)

    out = _final_norm(x2, final_norm.reshape(1, D_MODEL))
    return out.reshape(bsz, seq, D_MODEL)
```

```python
import functools
import math

import jax
import jax.numpy as jnp
import numpy as np
from jax import lax
from jax.experimental import pallas as pl
from jax.experimental.pallas import tpu as pltpu

F32 = jnp.float32
BF16 = jnp.bfloat16

D_MODEL = 2048
DEPTH = 4
CHUNK = 64
D_DIFF = 1024
HEAD_DIM = 128
QK_HALF = 64
N_HEADS = 8
QK_SCALE = 1.0 / math.sqrt(QK_HALF)
LOG2E = 1.4426950408889634
D_POOL = 512
POOL_WINDOWS = (2, 4, 8, 16)
POOL_GROUP = 128
CONV_CH = 512
CONV_K = 31
D_IN = 3 * D_DIFF + D_POOL + 2 * CONV_CH
D_MIX = D_DIFF + D_POOL + CONV_CH
D_FF = 5504
FFN_CONV_K = 3
RMS_EPS = 1e-6
LN_EPS = 1e-5

VMEM_LIMIT_BYTES = 56 * 1024 * 1024
SUBLANES = 8

IN_TM, IN_TN = 1024, 1536
ATT_T = 512
ATT_NQ = 256
ATT_ONES = 16
MIX_TS = 512
MIX_HALO = 32
OUT_TM, OUT_TN = 512, 2048
FFN_TM, FFN_TF = 512, 512
FFN_TC = 256
D_FF_PAD = ((D_FF + FFN_TF - 1) // FFN_TF) * FFN_TF


def _params(*sem):
    return pltpu.CompilerParams(dimension_semantics=sem, vmem_limit_bytes=VMEM_LIMIT_BYTES)


def _rms(x, g):
    return x * lax.rsqrt(jnp.mean(x * x, axis=-1, keepdims=True) + RMS_EPS) * g


def _in_proj_kernel(x_ref, g_ref, w_ref, o_ref, h_ref):
    @pl.when(pl.program_id(1) == 0)
    def _():
        h_ref[...] = _rms(x_ref[...], g_ref[...]).astype(BF16)

    o_ref[...] = jnp.dot(h_ref[...], w_ref[...], preferred_element_type=F32).astype(o_ref.dtype)


def _in_proj(x2, g, w):
    m = x2.shape[0]
    tm = min(IN_TM, m)
    return pl.pallas_call(
        _in_proj_kernel,
        grid=(m // tm, D_IN // IN_TN),
        in_specs=[
            pl.BlockSpec((tm, D_MODEL), lambda i, j: (i, 0)),
            pl.BlockSpec((1, D_MODEL), lambda i, j: (0, 0)),
            pl.BlockSpec((D_MODEL, IN_TN), lambda i, j: (0, j)),
        ],
        out_specs=pl.BlockSpec((tm, IN_TN), lambda i, j: (i, j)),
        out_shape=jax.ShapeDtypeStruct((m, D_IN), BF16),
        scratch_shapes=[pltpu.VMEM((tm, D_MODEL), BF16)],
        compiler_params=_params("parallel", "arbitrary"),
        name="in_proj",
    )(x2, g, w)


def _attn_tables(nb):
    qi, kj, first = [], [], []
    for i in range(nb):
        for idx, j in enumerate([i] + list(range(i))):
            qi.append(i)
            kj.append(j)
            first.append(1 if idx == 0 else 0)
    return np.asarray([qi, kj, first], np.int32)


def _attn_kernel(sc_ref, tab_ref, q_ref, k_ref, v_ref, pos_ref, gain_ref, o_ref,
                 kaug_ref, vt_ref, diag_ref, qa1_ref, qa2_ref, sa_ref, sb_ref,
                 m1_ref, acc1_ref, m2_ref, acc2_ref, *, n_blocks, n_steps, out_scale):
    t = ATT_T
    h = pl.program_id(1)
    lam = sc_ref[0]
    c_hi = sc_ref[1 + h]
    c_lo = sc_ref[1 + N_HEADS + h]
    row = lax.broadcasted_iota(jnp.int32, (HEAD_DIM, t), 0)

    for c in range(n_blocks):
        rows = pl.ds(c * t, t)
        kaug_ref[c, :, 0:HEAD_DIM] = k_ref[rows, :]
        kaug_ref[c, :, HEAD_DIM:2 * HEAD_DIM] = pos_ref[rows, :]
        vt_ref[c, 0:HEAD_DIM, :] = v_ref[rows, :].astype(F32).T.astype(BF16)
        vt_ref[c, HEAD_DIM:HEAD_DIM + ATT_ONES, :] = jnp.ones((ATT_ONES, t), BF16)
    kk = lax.broadcasted_iota(jnp.int32, (t, t), 0)
    tt = lax.broadcasted_iota(jnp.int32, (t, t), 1)
    allowed = (kk // CHUNK) <= (tt // CHUNK)
    future = jnp.where(kk > tt, (2.0 * (c_hi + c_lo)) * (tt - kk).astype(F32), 0.0)
    diag_ref[...] = jnp.where(allowed, future, -jnp.inf)
    coef = jnp.where(row == 0, c_hi * 256.0,
                     jnp.where(row == 1, c_hi,
                               jnp.where(row == 2, c_lo * 256.0,
                                         jnp.where(row == 3, c_lo, 0.0)))).astype(BF16)
    qa1_ref[HEAD_DIM:2 * HEAD_DIM, :] = coef
    qa2_ref[HEAD_DIM:2 * HEAD_DIM, :] = coef

    def block_rows(qi):
        return pl.ds(pl.multiple_of(qi * t, t), t)

    def prep_q(qi):
        qt = (q_ref[block_rows(qi), :].astype(F32) * (QK_SCALE * LOG2E)).T
        qa1_ref[0:HEAD_DIM, :] = jnp.where(row < QK_HALF, qt, 0.0).astype(BF16)
        qa2_ref[0:HEAD_DIM, :] = jnp.where(row >= QK_HALF, qt, 0.0).astype(BF16)

    def init_stats():
        for m_ref, acc_ref in ((m1_ref, acc1_ref), (m2_ref, acc2_ref)):
            m_ref[...] = jnp.full((1, t), -jnp.inf, F32)
            acc_ref[...] = jnp.zeros((HEAD_DIM + ATT_ONES, t), F32)

    units = [(mp, c) for c in range(t // ATT_NQ) for mp in range(2)]
    qa_refs = (qa1_ref, qa2_ref)
    acc_refs = (acc1_ref, acc2_ref)
    m_refs = (m1_ref, m2_ref)

    def cols(c):
        return slice(c * ATT_NQ, (c + 1) * ATT_NQ)

    def scores(n, s_ref, mp, c):
        kb = kaug_ref[tab_ref[1, n]]
        s_ref[mp, :, cols(c)] = jnp.dot(kb, qa_refs[mp][:, cols(c)], preferred_element_type=F32)

    def softmax(s_ref, mp, c, diagonal):
        def biased():
            s = s_ref[mp, :, cols(c)]
            return s + diag_ref[:, cols(c)] if diagonal else s

        m_old = m_refs[mp][:, cols(c)]
        m_new = jnp.maximum(m_old, jnp.max(biased(), axis=0, keepdims=True))
        m_refs[mp][:, cols(c)] = m_new
        return jnp.exp2(m_old - m_new), jnp.exp2(biased() - m_new).astype(BF16)

    def weighted_values(n, mp, c, alpha, p):
        vt = vt_ref[tab_ref[1, n]]
        acc_refs[mp][:, cols(c)] = (alpha * acc_refs[mp][:, cols(c)]
                                    + jnp.dot(vt, p, preferred_element_type=F32))

    def compute(n, s_ref):
        for mp, c in units:
            scores(n, s_ref, mp, c)

    def consume(n, s_cur, s_next, diagonal):
        pending = None
        for mp, c in units:
            if s_next is not None:
                scores(n + 1, s_next, mp, c)
            ap = softmax(s_cur, mp, c, diagonal)
            if pending is not None:
                weighted_values(n, *pending)
            pending = (mp, c) + ap
        weighted_values(n, *pending)

    def finalize(qi):
        o1 = acc1_ref[0:HEAD_DIM, :] / acc1_ref[HEAD_DIM:HEAD_DIM + 1, :]
        o2 = acc2_ref[0:HEAD_DIM, :] / acc2_ref[HEAD_DIM:HEAD_DIM + 1, :]
        o = o1 - lam * o2
        o = o * lax.rsqrt(jnp.mean(o * o, axis=0, keepdims=True) + RMS_EPS)
        o_ref[block_rows(qi), :] = (o.T * gain_ref[...] * out_scale).astype(o_ref.dtype)

    def consume_step(n, s_cur, s_next):
        @pl.when(tab_ref[2, n] == 1)
        def _():
            init_stats()
            consume(n, s_cur, s_next, True)

        @pl.when(tab_ref[2, n] != 1)
        def _():
            consume(n, s_cur, s_next, False)

    def pipelined_step(n, s_cur, s_next):
        next_first = tab_ref[2, n + 1] == 1

        @pl.when(next_first)
        def _():
            prep_q(tab_ref[0, n + 1])

        consume_step(n, s_cur, s_next)

        @pl.when(next_first)
        def _():
            finalize(tab_ref[0, n])

    prep_q(0)
    compute(0, sa_ref)

    def body(m, carry):
        pipelined_step(2 * m, sa_ref, sb_ref)
        pipelined_step(2 * m + 1, sb_ref, sa_ref)
        return carry

    n_pairs = (n_steps - 1) // 2
    lax.fori_loop(0, n_pairs, body, 0)
    s_last = sa_ref
    if (n_steps - 1) % 2 == 1:
        pipelined_step(n_steps - 2, sa_ref, sb_ref)
        s_last = sb_ref
    consume_step(n_steps - 1, s_last, None)
    finalize(n_blocks - 1)


def _attention(p3, scalars, posfeat, gain, lam_init):
    b, s, _ = p3.shape
    t = ATT_T
    nb = s // t
    tables = _attn_tables(nb)
    kern = functools.partial(_attn_kernel, n_blocks=nb, n_steps=tables.shape[1], out_scale=1.0 - lam_init)
    hb = D_DIFF // HEAD_DIM
    return pl.pallas_call(
        kern,
        grid=(b, N_HEADS),
        in_specs=[
            pl.BlockSpec(memory_space=pltpu.SMEM),
            pl.BlockSpec(memory_space=pltpu.SMEM),
            pl.BlockSpec((None, s, HEAD_DIM), lambda b_, h: (b_, 0, h)),
            pl.BlockSpec((None, s, HEAD_DIM), lambda b_, h: (b_, 0, hb + h)),
            pl.BlockSpec((None, s, HEAD_DIM), lambda b_, h: (b_, 0, 2 * hb + h)),
            pl.BlockSpec((s, HEAD_DIM), lambda b_, h: (0, 0)),
            pl.BlockSpec((None, 1, HEAD_DIM), lambda b_, h: (h, 0, 0)),
        ],
        out_specs=pl.BlockSpec((None, s, HEAD_DIM), lambda b_, h: (b_, 0, h)),
        out_shape=jax.ShapeDtypeStruct((b, s, D_DIFF), BF16),
        scratch_shapes=[
            pltpu.VMEM((nb, t, 2 * HEAD_DIM), BF16),
            pltpu.VMEM((nb, HEAD_DIM + ATT_ONES, t), BF16),
            pltpu.VMEM((t, t), F32),
            pltpu.VMEM((2 * HEAD_DIM, t), BF16),
            pltpu.VMEM((2 * HEAD_DIM, t), BF16),
            pltpu.VMEM((2, t, t), F32),
            pltpu.VMEM((2, t, t), F32),
            pltpu.VMEM((1, t), F32), pltpu.VMEM((HEAD_DIM + ATT_ONES, t), F32),
            pltpu.VMEM((1, t), F32), pltpu.VMEM((HEAD_DIM + ATT_ONES, t), F32),
        ],
        compiler_params=_params("parallel", "parallel"),
        name="diff_attention",
    )(scalars, jnp.asarray(tables), p3, p3, p3, posfeat, gain)


def _mixer_kernel(a_ref, c1_ref, c2_ref, pw_ref, ps_ref, dww_ref, dwb_ref, lng_ref, lnb_ref,
                  cpw_ref, cpb_ref, o_ref, aext_ref, gext_ref, conv_ref):
    ts = MIX_TS
    hl = MIX_HALO
    si = pl.program_id(1)

    @pl.when(si == 0)
    def _():
        aext_ref[0:hl, :] = jnp.zeros((hl, D_POOL), F32)
        gext_ref[0:hl, :] = jnp.zeros((hl, CONV_CH), F32)

    @pl.when(si != 0)
    def _():
        aext_ref[0:hl, :] = aext_ref[ts:ts + hl, :]
        gext_ref[0:hl, :] = gext_ref[ts:ts + hl, :]

    a = a_ref[...].astype(F32)
    aext_ref[hl:hl + ts, :] = a
    c1 = c1_ref[...].astype(F32)
    c2 = c2_ref[...].astype(F32)
    gext_ref[hl:hl + ts, :] = c1 * jax.nn.sigmoid(c2)

    tpos = si * ts + lax.broadcasted_iota(jnp.int32, (ts, 1), 0)
    for g, w in enumerate(POOL_WINDOWS):
        cols = slice(g * POOL_GROUP, (g + 1) * POOL_GROUP)
        acc = aext_ref[hl:hl + ts, cols]
        for d in range(1, w):
            acc = acc + aext_ref[hl - d:hl - d + ts, cols]
        cnt = jnp.minimum(tpos + 1, w).astype(F32)
        pooled = acc / cnt - aext_ref[hl:hl + ts, cols]
        y = jnp.dot(pooled.astype(BF16), pw_ref[g], preferred_element_type=F32)
        o_ref[:, cols] = (y * ps_ref[:, cols]).astype(o_ref.dtype)

    rc = 64
    for r0 in range(0, ts, rc):
        for c0 in range(0, CONV_CH, 128):
            cols = slice(c0, c0 + 128)
            acc = jnp.zeros((rc, 128), F32) + dwb_ref[:, cols]
            for j in range(CONV_K):
                off = hl + r0 - (CONV_K - 1) + j
                acc = acc + dww_ref[j:j + 1, cols] * gext_ref[off:off + rc, cols]
            conv_ref[r0:r0 + rc, cols] = acc

    u = conv_ref[...]
    mu = jnp.mean(u, axis=-1, keepdims=True)
    var = jnp.mean(jnp.square(u - mu), axis=-1, keepdims=True)
    z = (u - mu) * lax.rsqrt(var + LN_EPS) * lng_ref[...] + lnb_ref[...]
    z = z * jax.nn.sigmoid(z)
    y = jnp.dot(z.astype(BF16), cpw_ref[...], preferred_element_type=F32) + cpb_ref[...]
    o_ref[:, D_POOL:D_POOL + CONV_CH] = y.astype(o_ref.dtype)


def _mixer(p3, pool_w, pool_scale, dw_w, dw_b, ln_g, ln_b, pw_w, pw_b):
    b, s, _ = p3.shape
    ts = MIX_TS
    a_blk = (3 * D_DIFF) // D_POOL
    full = lambda shape: pl.BlockSpec(shape, lambda b_, i: (0,) * len(shape))
    return pl.pallas_call(
        _mixer_kernel,
        grid=(b, s // ts),
        in_specs=[
            pl.BlockSpec((None, ts, D_POOL), lambda b_, i: (b_, i, a_blk)),
            pl.BlockSpec((None, ts, CONV_CH), lambda b_, i: (b_, i, a_blk + 1)),
            pl.BlockSpec((None, ts, CONV_CH), lambda b_, i: (b_, i, a_blk + 2)),
            full((len(POOL_WINDOWS), POOL_GROUP, POOL_GROUP)),
            full((1, D_POOL)),
            full((CONV_K, CONV_CH)),
            full((1, CONV_CH)),
            full((1, CONV_CH)),
            full((1, CONV_CH)),
            full((CONV_CH, CONV_CH)),
            full((1, CONV_CH)),
        ],
        out_specs=pl.BlockSpec((None, ts, D_POOL + CONV_CH), lambda b_, i: (b_, i, 0)),
        out_shape=jax.ShapeDtypeStruct((b, s, D_POOL + CONV_CH), BF16),
        scratch_shapes=[
            pltpu.VMEM((MIX_HALO + ts, D_POOL), F32),
            pltpu.VMEM((MIX_HALO + ts, CONV_CH), F32),
            pltpu.VMEM((ts, CONV_CH), F32),
        ],
        compiler_params=_params("parallel", "arbitrary"),
        name="pool_conformer",
    )(p3, p3, p3, pool_w, pool_scale, dw_w, dw_b, ln_g, ln_b, pw_w, pw_b)


def _out_proj_kernel(x_ref, yb_ref, yac_ref, wb_ref, wac_ref, o_ref):
    acc = jnp.dot(yb_ref[...], wb_ref[...], preferred_element_type=F32)
    acc = acc + jnp.dot(yac_ref[...], wac_ref[...], preferred_element_type=F32)
    o_ref[...] = x_ref[...] + acc


def _out_proj(x2, yb, yac, w):
    m = x2.shape[0]
    tm = min(OUT_TM, m)
    tn = OUT_TN
    kb = D_DIFF
    return pl.pallas_call(
        _out_proj_kernel,
        grid=(m // tm, D_MODEL // tn),
        in_specs=[
            pl.BlockSpec((tm, tn), lambda i, j: (i, j)),
            pl.BlockSpec((tm, kb), lambda i, j: (i, 0)),
            pl.BlockSpec((tm, D_MIX - kb), lambda i, j: (i, 0)),
            pl.BlockSpec((kb, tn), lambda i, j: (0, j)),
            pl.BlockSpec((D_MIX - kb, tn), lambda i, j: (1, j)),
        ],
        out_specs=pl.BlockSpec((tm, tn), lambda i, j: (i, j)),
        out_shape=jax.ShapeDtypeStruct((m, D_MODEL), F32),
        compiler_params=_params("parallel", "arbitrary"),
        name="out_proj",
    )(x2, yb, yac, w, w)


def _ffn_kernel(x_ref, g_ref, wg_ref, wu_ref, cw_ref, cb_ref, wo_ref, o_ref,
                h_ref, gext_ref, halo_ref, *, tiles_per_seq):
    tm = FFN_TM
    mi = pl.program_id(0)
    f = pl.program_id(1)

    @pl.when(f == 0)
    def _():
        x = x_ref[...]
        h_ref[...] = _rms(x, g_ref[...]).astype(BF16)
        o_ref[...] = x

    first = (mi % tiles_per_seq) == 0

    @pl.when(first)
    def _():
        gext_ref[0:SUBLANES, :] = jnp.zeros((SUBLANES, FFN_TF), F32)

    @pl.when(jnp.logical_not(first))
    def _():
        gext_ref[0:SUBLANES, :] = halo_ref[f]

    h = h_ref[...]
    out = None
    for c0 in range(0, FFN_TF, FFN_TC):
        cols = slice(c0, c0 + FFN_TC)
        gate = jnp.dot(h, wg_ref[:, cols], preferred_element_type=F32)
        up = jnp.dot(h, wu_ref[:, cols], preferred_element_type=F32)
        gext_ref[SUBLANES:SUBLANES + tm, cols] = gate
        halo_ref[f, :, cols] = gate[tm - SUBLANES:, :]
        gx = gext_ref[:, cols]
        g1 = pltpu.roll(gx, 1, 0)[SUBLANES:, :]
        g2 = pltpu.roll(gx, 2, 0)[SUBLANES:, :]
        hg = cw_ref[0:1, cols] * g2 + cw_ref[1:2, cols] * g1 + cw_ref[2:3, cols] * gate + cb_ref[:, cols]
        act = ((hg + hg * jnp.tanh(hg)) * up).astype(BF16)
        part = jnp.dot(act, wo_ref[cols, :], preferred_element_type=F32)
        out = part if out is None else out + part
    o_ref[...] += out


def _ffn(x2, g, wg, wu, cw, cb, wo, seq):
    m = x2.shape[0]
    tm, tf = FFN_TM, FFN_TF
    nf = D_FF_PAD // tf
    kern = functools.partial(_ffn_kernel, tiles_per_seq=seq // tm)
    return pl.pallas_call(
        kern,
        grid=(m // tm, nf),
        in_specs=[
            pl.BlockSpec((tm, D_MODEL), lambda i, j: (i, 0)),
            pl.BlockSpec((1, D_MODEL), lambda i, j: (0, 0)),
            pl.BlockSpec((D_MODEL, tf), lambda i, j: (0, j)),
            pl.BlockSpec((D_MODEL, tf), lambda i, j: (0, j)),
            pl.BlockSpec((FFN_CONV_K, tf), lambda i, j: (0, j)),
            pl.BlockSpec((1, tf), lambda i, j: (0, j)),
            pl.BlockSpec((tf, D_MODEL), lambda i, j: (j, 0)),
        ],
        out_specs=pl.BlockSpec((tm, D_MODEL), lambda i, j: (i, 0)),
        out_shape=jax.ShapeDtypeStruct((m, D_MODEL), F32),
        scratch_shapes=[
            pltpu.VMEM((tm, D_MODEL), BF16),
            pltpu.VMEM((SUBLANES + tm, tf), F32),
            pltpu.VMEM((nf, SUBLANES, tf), F32),
        ],
        compiler_params=_params("arbitrary", "arbitrary"),
        name="convglu_ffn",
    )(x2, g, wg, wu, cw, cb, wo)


def _final_norm_kernel(x_ref, g_ref, o_ref):
    o_ref[...] = _rms(x_ref[...], g_ref[...])


def _final_norm(x2, g):
    m = x2.shape[0]
    tm = min(1024, m)
    return pl.pallas_call(
        _final_norm_kernel,
        grid=(m // tm,),
        in_specs=[pl.BlockSpec((tm, D_MODEL), lambda i: (i, 0)),
                  pl.BlockSpec((1, D_MODEL), lambda i: (0, 0))],
        out_specs=pl.BlockSpec((tm, D_MODEL), lambda i: (i, 0)),
        out_shape=jax.ShapeDtypeStruct((m, D_MODEL), F32),
        compiler_params=_params("parallel"),
        name="final_norm",
    )(x2, g)


def _pad_ff(a, axis):
    pad = [(0, 0)] * a.ndim
    pad[axis] = (0, D_FF_PAD - D_FF)
    return jnp.pad(a, pad)


def kernel(x, norm_mix, w_in, lam_q1, lam_k1, lam_q2, lam_k2, diff_head_gain, pool_w, pool_scale,
           conv_dw_w, conv_dw_b, conv_ln_g, conv_ln_b, conv_pw_w, conv_pw_b, w_out, norm_ffn,
           w_ffn_in, ffn_conv_w, ffn_conv_b, w_ffn_out, final_norm):
    bsz, seq, _ = x.shape
    m = bsz * seq
    assert seq % MIX_TS == 0 and seq % ATT_T == 0 and seq % FFN_TM == 0 and seq <= 256 * 256

    coef = np.asarray([2.0 ** (-8.0 * (h + 1.0) / N_HEADS) * LOG2E for h in range(N_HEADS)], np.float64)
    coef_hi = coef.astype(BF16).astype(np.float64)
    coef_lo = (coef - coef_hi).astype(BF16).astype(np.float64)
    coefs = jnp.asarray(np.concatenate([coef_hi, coef_lo]), F32)
    kpos = np.arange(seq)
    posfeat = np.zeros((seq, HEAD_DIM), np.float32)
    posfeat[:, 0] = posfeat[:, 2] = kpos // 256
    posfeat[:, 1] = posfeat[:, 3] = kpos % 256
    posfeat = jnp.asarray(posfeat, BF16)

    x2 = x.reshape(m, D_MODEL)
    for l in range(DEPTH):
        lam_init = 0.8 - 0.6 * math.exp(-0.3 * l)
        lam = (jnp.exp(jnp.sum(lam_q1[l] * lam_k1[l])) - jnp.exp(jnp.sum(lam_q2[l] * lam_k2[l])) + lam_init)
        scalars = jnp.concatenate([lam.reshape(1).astype(F32), coefs])

        p = _in_proj(x2, norm_mix[l].reshape(1, D_MODEL), w_in[l].astype(BF16))
        p3 = p.reshape(bsz, seq, D_IN)
        y_b = _attention(p3, scalars, posfeat, diff_head_gain[l].reshape(N_HEADS, 1, HEAD_DIM), lam_init)
        y_ac = _mixer(p3, pool_w[l].astype(BF16), pool_scale[l].reshape(1, D_POOL), conv_dw_w[l],
                      conv_dw_b[l].reshape(1, CONV_CH), conv_ln_g[l].reshape(1, CONV_CH),
                      conv_ln_b[l].reshape(1, CONV_CH), conv_pw_w[l].astype(BF16),
                      conv_pw_b[l].reshape(1, CONV_CH))
        x2 = _out_proj(x2, y_b.reshape(m, D_DIFF), y_ac.reshape(m, D_POOL + CONV_CH), w_out[l].astype(BF16))

        wg = _pad_ff(w_ffn_in[l][:, :D_FF].astype(BF16), 1)
        wu = _pad_ff(w_ffn_in[l][:, D_FF:].astype(BF16), 1)
        cw = _pad_ff(0.5 * ffn_conv_w[l], 1)
        cb = _pad_ff(0.5 * ffn_conv_b[l].reshape(1, D_FF), 1)
        wo = _pad_ff(w_ffn_out[l].astype(BF16), 0)
        x2 = _ffn(x2, norm_ffn[l].reshape(1, D_MODEL), wg, wu, cw, cb, wo, seq)

    out = _final_norm(x2, final_norm.reshape(1, D_MODEL))
    return out.reshape(bsz, seq, D_MODEL)
```

```python
import functools
import math

import jax
import jax.numpy as jnp
import numpy as np
from jax import lax
from jax.experimental import pallas as pl
from jax.experimental.pallas import tpu as pltpu

F32 = jnp.float32
BF16 = jnp.bfloat16

D_MODEL = 2048
DEPTH = 4
CHUNK = 64
D_DIFF = 1024
HEAD_DIM = 128
QK_HALF = 64
N_HEADS = 8
QK_SCALE = 1.0 / math.sqrt(QK_HALF)
LOG2E = 1.4426950408889634
D_POOL = 512
POOL_WINDOWS = (2, 4, 8, 16)
POOL_GROUP = 128
CONV_CH = 512
CONV_K = 31
D_IN = 3 * D_DIFF + D_POOL + 2 * CONV_CH
D_MIX = D_DIFF + D_POOL + CONV_CH
D_FF = 5504
FFN_CONV_K = 3
RMS_EPS = 1e-6
LN_EPS = 1e-5

VMEM_LIMIT_BYTES = 56 * 1024 * 1024
SUBLANES = 8

IN_TM, IN_TN = 1024, 1536
ATT_T = 512
ATT_NQ = 256
ATT_ONES = 16
MIX_TS = 512
MIX_HALO = 32
MIX_RC = 64
OUT_TM, OUT_TN = 512, 2048
FFN_TM, FFN_TF = 512, 512
FFN_TC = 256
D_FF_PAD = ((D_FF + FFN_TF - 1) // FFN_TF) * FFN_TF


def _params(*sem):
    return pltpu.CompilerParams(dimension_semantics=sem, vmem_limit_bytes=VMEM_LIMIT_BYTES)


def _rms(x, g):
    return x * lax.rsqrt(jnp.mean(x * x, axis=-1, keepdims=True) + RMS_EPS) * g


def _in_proj_kernel(x_ref, g_ref, w_ref, o_ref, h_ref):
    @pl.when(pl.program_id(1) == 0)
    def _():
        h_ref[...] = _rms(x_ref[...], g_ref[...]).astype(BF16)

    o_ref[...] = jnp.dot(h_ref[...], w_ref[...], preferred_element_type=F32).astype(o_ref.dtype)


def _in_proj(x2, g, w):
    m = x2.shape[0]
    tm = min(IN_TM, m)
    return pl.pallas_call(
        _in_proj_kernel,
        grid=(m // tm, D_IN // IN_TN),
        in_specs=[
            pl.BlockSpec((tm, D_MODEL), lambda i, j: (i, 0)),
            pl.BlockSpec((1, D_MODEL), lambda i, j: (0, 0)),
            pl.BlockSpec((D_MODEL, IN_TN), lambda i, j: (0, j)),
        ],
        out_specs=pl.BlockSpec((tm, IN_TN), lambda i, j: (i, j)),
        out_shape=jax.ShapeDtypeStruct((m, D_IN), BF16),
        scratch_shapes=[pltpu.VMEM((tm, D_MODEL), BF16)],
        compiler_params=_params("parallel", "arbitrary"),
        name="in_proj",
    )(x2, g, w)


def _attn_tables(nb):
    qi, kj, first = [], [], []
    for i in range(nb):
        for idx, j in enumerate([i] + list(range(i))):
            qi.append(i)
            kj.append(j)
            first.append(1 if idx == 0 else 0)
    return np.asarray([qi, kj, first], np.int32)


def _attn_kernel(sc_ref, tab_ref, q_ref, k_ref, v_ref, pos_ref, gain_ref, o_ref,
                 kaug_ref, vt_ref, diag_ref, qa1_ref, qa2_ref, sa_ref, sb_ref,
                 m1_ref, acc1_ref, m2_ref, acc2_ref, *, n_blocks, n_steps, out_scale):
    t = ATT_T
    h = pl.program_id(1)
    lam = sc_ref[0]
    c_hi = sc_ref[1 + h]
    c_lo = sc_ref[1 + N_HEADS + h]
    row = lax.broadcasted_iota(jnp.int32, (HEAD_DIM, t), 0)

    for c in range(n_blocks):
        rows = pl.ds(c * t, t)
        kaug_ref[c, :, 0:HEAD_DIM] = k_ref[rows, :]
        kaug_ref[c, :, HEAD_DIM:2 * HEAD_DIM] = pos_ref[rows, :]
        vt_ref[c, 0:HEAD_DIM, :] = v_ref[rows, :].astype(F32).T.astype(BF16)
        vt_ref[c, HEAD_DIM:HEAD_DIM + ATT_ONES, :] = jnp.ones((ATT_ONES, t), BF16)
    kk = lax.broadcasted_iota(jnp.int32, (t, t), 0)
    tt = lax.broadcasted_iota(jnp.int32, (t, t), 1)
    allowed = (kk // CHUNK) <= (tt // CHUNK)
    future = jnp.where(kk > tt, (2.0 * (c_hi + c_lo)) * (tt - kk).astype(F32), 0.0)
    diag_ref[...] = jnp.where(allowed, future, -jnp.inf)
    coef = jnp.where(row == 0, c_hi * 256.0,
                     jnp.where(row == 1, c_hi,
                               jnp.where(row == 2, c_lo * 256.0,
                                         jnp.where(row == 3, c_lo, 0.0)))).astype(BF16)
    qa1_ref[HEAD_DIM:2 * HEAD_DIM, :] = coef
    qa2_ref[HEAD_DIM:2 * HEAD_DIM, :] = coef

    def block_rows(qi):
        return pl.ds(pl.multiple_of(qi * t, t), t)

    def prep_q(qi):
        qt = (q_ref[block_rows(qi), :].astype(F32) * (QK_SCALE * LOG2E)).T
        qa1_ref[0:HEAD_DIM, :] = jnp.where(row < QK_HALF, qt, 0.0).astype(BF16)
        qa2_ref[0:HEAD_DIM, :] = jnp.where(row >= QK_HALF, qt, 0.0).astype(BF16)

    def init_stats():
        for m_ref, acc_ref in ((m1_ref, acc1_ref), (m2_ref, acc2_ref)):
            m_ref[...] = jnp.full((1, t), -jnp.inf, F32)
            acc_ref[...] = jnp.zeros((HEAD_DIM + ATT_ONES, t), F32)

    units = [(mp, c) for c in range(t // ATT_NQ) for mp in range(2)]
    qa_refs = (qa1_ref, qa2_ref)
    acc_refs = (acc1_ref, acc2_ref)
    m_refs = (m1_ref, m2_ref)

    def cols(c):
        return slice(c * ATT_NQ, (c + 1) * ATT_NQ)

    def scores(n, s_ref, mp, c):
        kb = kaug_ref[tab_ref[1, n]]
        s_ref[mp, :, cols(c)] = jnp.dot(kb, qa_refs[mp][:, cols(c)], preferred_element_type=F32)

    def softmax(s_ref, mp, c, diagonal):
        def biased():
            s = s_ref[mp, :, cols(c)]
            return s + diag_ref[:, cols(c)] if diagonal else s

        m_old = m_refs[mp][:, cols(c)]
        m_new = jnp.maximum(m_old, jnp.max(biased(), axis=0, keepdims=True))
        m_refs[mp][:, cols(c)] = m_new
        return jnp.exp2(m_old - m_new), jnp.exp2(biased() - m_new).astype(BF16)

    def weighted_values(n, mp, c, alpha, p):
        vt = vt_ref[tab_ref[1, n]]
        acc_refs[mp][:, cols(c)] = (alpha * acc_refs[mp][:, cols(c)]
                                    + jnp.dot(vt, p, preferred_element_type=F32))

    def compute(n, s_ref):
        for mp, c in units:
            scores(n, s_ref, mp, c)

    def consume(n, s_cur, s_next, diagonal):
        pending = None
        for mp, c in units:
            if s_next is not None:
                scores(n + 1, s_next, mp, c)
            ap = softmax(s_cur, mp, c, diagonal)
            if pending is not None:
                weighted_values(n, *pending)
            pending = (mp, c) + ap
        weighted_values(n, *pending)

    def finalize(qi):
        o1 = acc1_ref[0:HEAD_DIM, :] / acc1_ref[HEAD_DIM:HEAD_DIM + 1, :]
        o2 = acc2_ref[0:HEAD_DIM, :] / acc2_ref[HEAD_DIM:HEAD_DIM + 1, :]
        o = o1 - lam * o2
        o = o * lax.rsqrt(jnp.mean(o * o, axis=0, keepdims=True) + RMS_EPS)
        o_ref[block_rows(qi), :] = (o.T * gain_ref[...] * out_scale).astype(o_ref.dtype)

    def consume_step(n, s_cur, s_next):
        @pl.when(tab_ref[2, n] == 1)
        def _():
            init_stats()
            consume(n, s_cur, s_next, True)

        @pl.when(tab_ref[2, n] != 1)
        def _():
            consume(n, s_cur, s_next, False)

    def pipelined_step(n, s_cur, s_next):
        next_first = tab_ref[2, n + 1] == 1

        @pl.when(next_first)
        def _():
            prep_q(tab_ref[0, n + 1])

        consume_step(n, s_cur, s_next)

        @pl.when(next_first)
        def _():
            finalize(tab_ref[0, n])

    prep_q(0)
    compute(0, sa_ref)

    def body(m, carry):
        pipelined_step(2 * m, sa_ref, sb_ref)
        pipelined_step(2 * m + 1, sb_ref, sa_ref)
        return carry

    n_pairs = (n_steps - 1) // 2
    lax.fori_loop(0, n_pairs, body, 0)
    s_last = sa_ref
    if (n_steps - 1) % 2 == 1:
        pipelined_step(n_steps - 2, sa_ref, sb_ref)
        s_last = sb_ref
    consume_step(n_steps - 1, s_last, None)
    finalize(n_blocks - 1)


def _attention(p3, scalars, posfeat, gain, lam_init):
    b, s, _ = p3.shape
    t = ATT_T
    nb = s // t
    tables = _attn_tables(nb)
    kern = functools.partial(_attn_kernel, n_blocks=nb, n_steps=tables.shape[1], out_scale=1.0 - lam_init)
    hb = D_DIFF // HEAD_DIM
    return pl.pallas_call(
        kern,
        grid=(b, N_HEADS),
        in_specs=[
            pl.BlockSpec(memory_space=pltpu.SMEM),
            pl.BlockSpec(memory_space=pltpu.SMEM),
            pl.BlockSpec((None, s, HEAD_DIM), lambda b_, h: (b_, 0, h)),
            pl.BlockSpec((None, s, HEAD_DIM), lambda b_, h: (b_, 0, hb + h)),
            pl.BlockSpec((None, s, HEAD_DIM), lambda b_, h: (b_, 0, 2 * hb + h)),
            pl.BlockSpec((s, HEAD_DIM), lambda b_, h: (0, 0)),
            pl.BlockSpec((None, 1, HEAD_DIM), lambda b_, h: (h, 0, 0)),
        ],
        out_specs=pl.BlockSpec((None, s, HEAD_DIM), lambda b_, h: (b_, 0, h)),
        out_shape=jax.ShapeDtypeStruct((b, s, D_DIFF), BF16),
        scratch_shapes=[
            pltpu.VMEM((nb, t, 2 * HEAD_DIM), BF16),
            pltpu.VMEM((nb, HEAD_DIM + ATT_ONES, t), BF16),
            pltpu.VMEM((t, t), F32),
            pltpu.VMEM((2 * HEAD_DIM, t), BF16),
            pltpu.VMEM((2 * HEAD_DIM, t), BF16),
            pltpu.VMEM((2, t, t), F32),
            pltpu.VMEM((2, t, t), F32),
            pltpu.VMEM((1, t), F32), pltpu.VMEM((HEAD_DIM + ATT_ONES, t), F32),
            pltpu.VMEM((1, t), F32), pltpu.VMEM((HEAD_DIM + ATT_ONES, t), F32),
        ],
        compiler_params=_params("parallel", "parallel"),
        name="diff_attention",
    )(scalars, jnp.asarray(tables), p3, p3, p3, posfeat, gain)


def _mixer_kernel(a_ref, c1_ref, c2_ref, pw_ref, ps_ref, dww_ref, dwb_ref, lng_ref, lnb_ref,
                  cpw_ref, cpb_ref, o_ref, aext_ref, gext_ref, conv_ref):
    ts = MIX_TS
    hl = MIX_HALO
    si = pl.program_id(1)

    @pl.when(si == 0)
    def _():
        aext_ref[0:hl, :] = jnp.zeros((hl, D_POOL), F32)
        gext_ref[0:hl, :] = jnp.zeros((hl, CONV_CH), F32)

    @pl.when(si != 0)
    def _():
        aext_ref[0:hl, :] = aext_ref[ts:ts + hl, :]
        gext_ref[0:hl, :] = gext_ref[ts:ts + hl, :]

    aext_ref[hl:hl + ts, :] = a_ref[...].astype(F32)
    half_c1 = 0.5 * c1_ref[...].astype(F32)
    gext_ref[hl:hl + ts, :] = half_c1 + half_c1 * jnp.tanh(0.5 * c2_ref[...].astype(F32))

    tpos = si * ts + lax.broadcasted_iota(jnp.int32, (ts, 1), 0)
    for g, w in enumerate(POOL_WINDOWS):
        cols = slice(g * POOL_GROUP, (g + 1) * POOL_GROUP)
        x = aext_ref[:, cols]
        acc, span = x, 1
        while span < w:
            acc = acc + pltpu.roll(acc, span, 0)
            span *= 2
        cnt = jnp.minimum(tpos + 1, w).astype(F32)
        pooled = acc[hl:, :] / cnt - x[hl:, :]
        y = jnp.dot(pooled.astype(BF16), pw_ref[g], preferred_element_type=F32)
        o_ref[:, cols] = (y * ps_ref[:, cols]).astype(o_ref.dtype)

    rc = MIX_RC
    for r0 in range(0, ts, rc):
        for c0 in range(0, CONV_CH, 128):
            cols = slice(c0, c0 + 128)
            acc = jnp.zeros((rc, 128), F32) + dwb_ref[:, cols]
            for b in range(SUBLANES):
                part = None
                for a8 in range((CONV_K - 1 - b) // SUBLANES + 1):
                    d = SUBLANES * a8 + b
                    lo = hl + r0 - SUBLANES * a8 - (SUBLANES if b else 0)
                    win = gext_ref[lo:hl + r0 - SUBLANES * a8 + rc, cols]
                    term = dww_ref[CONV_K - 1 - d:CONV_K - d, cols] * win
                    part = term if part is None else part + term
                acc = acc + (pltpu.roll(part, b, 0)[SUBLANES:, :] if b else part)
            conv_ref[r0:r0 + rc, cols] = acc

    u = conv_ref[...]
    mu = jnp.mean(u, axis=-1, keepdims=True)
    var = jnp.mean(jnp.square(u - mu), axis=-1, keepdims=True)
    hz = (u - mu) * lax.rsqrt(var + LN_EPS) * lng_ref[...] + lnb_ref[...]
    z = hz + hz * jnp.tanh(hz)
    y = jnp.dot(z.astype(BF16), cpw_ref[...], preferred_element_type=F32) + cpb_ref[...]
    o_ref[:, D_POOL:D_POOL + CONV_CH] = y.astype(o_ref.dtype)


def _mixer(p3, pool_w, pool_scale, dw_w, dw_b, ln_g, ln_b, pw_w, pw_b):
    b, s, _ = p3.shape
    ts = MIX_TS
    a_blk = (3 * D_DIFF) // D_POOL
    full = lambda shape: pl.BlockSpec(shape, lambda b_, i: (0,) * len(shape))
    return pl.pallas_call(
        _mixer_kernel,
        grid=(b, s // ts),
        in_specs=[
            pl.BlockSpec((None, ts, D_POOL), lambda b_, i: (b_, i, a_blk)),
            pl.BlockSpec((None, ts, CONV_CH), lambda b_, i: (b_, i, a_blk + 1)),
            pl.BlockSpec((None, ts, CONV_CH), lambda b_, i: (b_, i, a_blk + 2)),
            full((len(POOL_WINDOWS), POOL_GROUP, POOL_GROUP)),
            full((1, D_POOL)),
            full((CONV_K, CONV_CH)),
            full((1, CONV_CH)),
            full((1, CONV_CH)),
            full((1, CONV_CH)),
            full((CONV_CH, CONV_CH)),
            full((1, CONV_CH)),
        ],
        out_specs=pl.BlockSpec((None, ts, D_POOL + CONV_CH), lambda b_, i: (b_, i, 0)),
        out_shape=jax.ShapeDtypeStruct((b, s, D_POOL + CONV_CH), BF16),
        scratch_shapes=[
            pltpu.VMEM((MIX_HALO + ts, D_POOL), F32),
            pltpu.VMEM((MIX_HALO + ts, CONV_CH), F32),
            pltpu.VMEM((ts, CONV_CH), F32),
        ],
        compiler_params=_params("parallel", "arbitrary"),
        name="pool_conformer",
    )(p3, p3, p3, pool_w, pool_scale, dw_w, dw_b, ln_g, ln_b, pw_w, pw_b)


def _out_proj_kernel(x_ref, yb_ref, yac_ref, wb_ref, wac_ref, o_ref):
    acc = jnp.dot(yb_ref[...], wb_ref[...], preferred_element_type=F32)
    acc = acc + jnp.dot(yac_ref[...], wac_ref[...], preferred_element_type=F32)
    o_ref[...] = x_ref[...] + acc


def _out_proj(x2, yb, yac, w):
    m = x2.shape[0]
    tm = min(OUT_TM, m)
    tn = OUT_TN
    kb = D_DIFF
    return pl.pallas_call(
        _out_proj_kernel,
        grid=(m // tm, D_MODEL // tn),
        in_specs=[
            pl.BlockSpec((tm, tn), lambda i, j: (i, j)),
            pl.BlockSpec((tm, kb), lambda i, j: (i, 0)),
            pl.BlockSpec((tm, D_MIX - kb), lambda i, j: (i, 0)),
            pl.BlockSpec((kb, tn), lambda i, j: (0, j)),
            pl.BlockSpec((D_MIX - kb, tn), lambda i, j: (1, j)),
        ],
        out_specs=pl.BlockSpec((tm, tn), lambda i, j: (i, j)),
        out_shape=jax.ShapeDtypeStruct((m, D_MODEL), F32),
        compiler_params=_params("parallel", "arbitrary"),
        name="out_proj",
    )(x2, yb, yac, w, w)


def _ffn_kernel(*refs, tiles_per_seq, n_tiles, final_norm):
    x_ref, g_ref, wg_ref, wu_ref, cw_ref, cb_ref, wo_ref = refs[:7]
    fg_ref = refs[7] if final_norm else None
    o_ref, h_ref, gext_ref, halo_ref = refs[-4:]
    tm = FFN_TM
    mi = pl.program_id(0)
    f = pl.program_id(1)

    @pl.when(f == 0)
    def _():
        x = x_ref[...]
        h_ref[...] = _rms(x, g_ref[...]).astype(BF16)
        o_ref[...] = x

    first = (mi % tiles_per_seq) == 0

    @pl.when(first)
    def _():
        gext_ref[0:SUBLANES, :] = jnp.zeros((SUBLANES, FFN_TF), F32)

    @pl.when(jnp.logical_not(first))
    def _():
        gext_ref[0:SUBLANES, :] = halo_ref[f]

    h = h_ref[...]
    out = None
    for c0 in range(0, FFN_TF, FFN_TC):
        cols = slice(c0, c0 + FFN_TC)
        gate = jnp.dot(h, wg_ref[:, cols], preferred_element_type=F32)
        up = jnp.dot(h, wu_ref[:, cols], preferred_element_type=F32)
        gext_ref[SUBLANES:SUBLANES + tm, cols] = gate
        halo_ref[f, :, cols] = gate[tm - SUBLANES:, :]
        gx = gext_ref[:, cols]
        g1 = pltpu.roll(gx, 1, 0)[SUBLANES:, :]
        g2 = pltpu.roll(gx, 2, 0)[SUBLANES:, :]
        hg = cw_ref[0:1, cols] * g2 + cw_ref[1:2, cols] * g1 + cw_ref[2:3, cols] * gate + cb_ref[:, cols]
        act = ((hg + hg * jnp.tanh(hg)) * up).astype(BF16)
        part = jnp.dot(act, wo_ref[cols, :], preferred_element_type=F32)
        out = part if out is None else out + part
    o_ref[...] += out

    if final_norm:
        @pl.when(f == n_tiles - 1)
        def _():
            o_ref[...] = _rms(o_ref[...], fg_ref[...])


def _ffn(x2, g, wg, wu, cw, cb, wo, seq, final_gain=None):
    m = x2.shape[0]
    tm, tf = FFN_TM, FFN_TF
    nf = D_FF_PAD // tf
    kern = functools.partial(_ffn_kernel, tiles_per_seq=seq // tm, n_tiles=nf, final_norm=final_gain is not None)
    in_specs = [
        pl.BlockSpec((tm, D_MODEL), lambda i, j: (i, 0)),
        pl.BlockSpec((1, D_MODEL), lambda i, j: (0, 0)),
        pl.BlockSpec((D_MODEL, tf), lambda i, j: (0, j)),
        pl.BlockSpec((D_MODEL, tf), lambda i, j: (0, j)),
        pl.BlockSpec((FFN_CONV_K, tf), lambda i, j: (0, j)),
        pl.BlockSpec((1, tf), lambda i, j: (0, j)),
        pl.BlockSpec((tf, D_MODEL), lambda i, j: (j, 0)),
    ]
    args = [x2, g, wg, wu, cw, cb, wo]
    if final_gain is not None:
        in_specs.append(pl.BlockSpec((1, D_MODEL), lambda i, j: (0, 0)))
        args.append(final_gain)
    return pl.pallas_call(
        kern,
        grid=(m // tm, nf),
        in_specs=in_specs,
        out_specs=pl.BlockSpec((tm, D_MODEL), lambda i, j: (i, 0)),
        out_shape=jax.ShapeDtypeStruct((m, D_MODEL), F32),
        scratch_shapes=[
            pltpu.VMEM((tm, D_MODEL), BF16),
            pltpu.VMEM((SUBLANES + tm, tf), F32),
            pltpu.VMEM((nf, SUBLANES, tf), F32),
        ],
        compiler_params=_params("arbitrary", "arbitrary"),
        name="convglu_ffn",
    )(*args)


def _pad_ff(a, axis):
    pad = [(0, 0)] * a.ndim
    pad[axis] = (0, D_FF_PAD - D_FF)
    return jnp.pad(a, pad)


def kernel(x, norm_mix, w_in, lam_q1, lam_k1, lam_q2, lam_k2, diff_head_gain, pool_w, pool_scale,
           conv_dw_w, conv_dw_b, conv_ln_g, conv_ln_b, conv_pw_w, conv_pw_b, w_out, norm_ffn,
           w_ffn_in, ffn_conv_w, ffn_conv_b, w_ffn_out, final_norm):
    bsz, seq, _ = x.shape
    m = bsz * seq
    assert seq % MIX_TS == 0 and seq % ATT_T == 0 and seq % FFN_TM == 0 and seq <= 256 * 256

    coef = np.asarray([2.0 ** (-8.0 * (h + 1.0) / N_HEADS) * LOG2E for h in range(N_HEADS)], np.float64)
    coef_hi = coef.astype(BF16).astype(np.float64)
    coef_lo = (coef - coef_hi).astype(BF16).astype(np.float64)
    coefs = jnp.asarray(np.concatenate([coef_hi, coef_lo]), F32)
    kpos = np.arange(seq)
    posfeat = np.zeros((seq, HEAD_DIM), np.float32)
    posfeat[:, 0] = posfeat[:, 2] = kpos // 256
    posfeat[:, 1] = posfeat[:, 3] = kpos % 256
    posfeat = jnp.asarray(posfeat, BF16)

    x2 = x.reshape(m, D_MODEL)
    for l in range(DEPTH):
        lam_init = 0.8 - 0.6 * math.exp(-0.3 * l)
        lam = (jnp.exp(jnp.sum(lam_q1[l] * lam_k1[l])) - jnp.exp(jnp.sum(lam_q2[l] * lam_k2[l])) + lam_init)
        scalars = jnp.concatenate([lam.reshape(1).astype(F32), coefs])

        p = _in_proj(x2, norm_mix[l].reshape(1, D_MODEL), w_in[l].astype(BF16))
        p3 = p.reshape(bsz, seq, D_IN)
        y_b = _attention(p3, scalars, posfeat, diff_head_gain[l].reshape(N_HEADS, 1, HEAD_DIM), lam_init)
        y_ac = _mixer(p3, pool_w[l].astype(BF16), pool_scale[l].reshape(1, D_POOL), conv_dw_w[l],
                      conv_dw_b[l].reshape(1, CONV_CH), 0.5 * conv_ln_g[l].reshape(1, CONV_CH),
                      0.5 * conv_ln_b[l].reshape(1, CONV_CH), conv_pw_w[l].astype(BF16),
                      conv_pw_b[l].reshape(1, CONV_CH))
        x2 = _out_proj(x2, y_b.reshape(m, D_DIFF), y_ac.reshape(m, D_POOL + CONV_CH), w_out[l].astype(BF16))

        wg = _pad_ff(w_ffn_in[l][:, :D_FF].astype(BF16), 1)
        wu = _pad_ff(w_ffn_in[l][:, D_FF:].astype(BF16), 1)
        cw = _pad_ff(0.5 * ffn_conv_w[l], 1)
        cb = _pad_ff(0.5 * ffn_conv_b[l].reshape(1, D_FF), 1)
        wo = _pad_ff(w_ffn_out[l].astype(BF16), 0)
        final_gain = final_norm.reshape(1, D_MODEL) if l == DEPTH - 1 else None
        x2 = _ffn(x2, norm_ffn[l].reshape(1, D_MODEL), wg, wu, cw, cb, wo, seq, final_gain)

    return x2.reshape(bsz, seq, D_MODEL)
```

```python
import functools
import math

import jax
import jax.numpy as jnp
import numpy as np
from jax import lax
from jax.experimental import pallas as pl
from jax.experimental.pallas import tpu as pltpu

F32 = jnp.float32
BF16 = jnp.bfloat16

D_MODEL = 2048
DEPTH = 4
CHUNK = 64
D_DIFF = 1024
HEAD_DIM = 128
QK_HALF = 64
N_HEADS = 8
QK_SCALE = 1.0 / math.sqrt(QK_HALF)
LOG2E = 1.4426950408889634
D_POOL = 512
POOL_WINDOWS = (2, 4, 8, 16)
POOL_GROUP = 128
CONV_CH = 512
CONV_K = 31
D_IN = 3 * D_DIFF + D_POOL + 2 * CONV_CH
D_MIX = D_DIFF + D_POOL + CONV_CH
D_FF = 5504
FFN_CONV_K = 3
RMS_EPS = 1e-6
LN_EPS = 1e-5

VMEM_LIMIT_BYTES = 56 * 1024 * 1024
SUBLANES = 8

IN_TM, IN_TN = 1024, 1536
ATT_T = 512
ATT_NQ = 256
ATT_ONES = 16
MIX_TS = 512
MIX_HALO = 32
MIX_RC = 64
OUT_TM, OUT_TN = 512, 2048
FFN_TM, FFN_TF = 512, 512
FFN_TC = 256
D_FF_PAD = ((D_FF + FFN_TF - 1) // FFN_TF) * FFN_TF


def _params(*sem):
    return pltpu.CompilerParams(dimension_semantics=sem, vmem_limit_bytes=VMEM_LIMIT_BYTES)


def _rms(x, g):
    return x * lax.rsqrt(jnp.mean(x * x, axis=-1, keepdims=True) + RMS_EPS) * g


def _in_proj_kernel(x_ref, g_ref, w_ref, o_ref, h_ref):
    @pl.when(pl.program_id(1) == 0)
    def _():
        h_ref[...] = _rms(x_ref[...], g_ref[...]).astype(BF16)

    o_ref[...] = jnp.dot(h_ref[...], w_ref[...], preferred_element_type=F32).astype(o_ref.dtype)


def _in_proj(x2, g, w):
    m = x2.shape[0]
    tm = min(IN_TM, m)
    return pl.pallas_call(
        _in_proj_kernel,
        grid=(m // tm, D_IN // IN_TN),
        in_specs=[
            pl.BlockSpec((tm, D_MODEL), lambda i, j: (i, 0)),
            pl.BlockSpec((1, D_MODEL), lambda i, j: (0, 0)),
            pl.BlockSpec((D_MODEL, IN_TN), lambda i, j: (0, j)),
        ],
        out_specs=pl.BlockSpec((tm, IN_TN), lambda i, j: (i, j)),
        out_shape=jax.ShapeDtypeStruct((m, D_IN), BF16),
        scratch_shapes=[pltpu.VMEM((tm, D_MODEL), BF16)],
        compiler_params=_params("parallel", "arbitrary"),
        name="in_proj",
    )(x2, g, w)


def _attn_tables(nb):
    qi, kj, first = [], [], []
    for i in range(nb):
        for idx, j in enumerate([i] + list(range(i))):
            qi.append(i)
            kj.append(j)
            first.append(1 if idx == 0 else 0)
    return np.asarray([qi, kj, first], np.int32)


def _attn_kernel(sc_ref, tab_ref, q_ref, k_ref, v_ref, pos_ref, gain_ref, o_ref,
                 kaug_ref, vt_ref, diag_ref, qa1_ref, qa2_ref, sa_ref, sb_ref,
                 m1_ref, acc1_ref, m2_ref, acc2_ref, *, n_blocks, n_steps, out_scale):
    t = ATT_T
    h = pl.program_id(1)
    lam = sc_ref[0]
    c_hi = sc_ref[1 + h]
    c_lo = sc_ref[1 + N_HEADS + h]
    row = lax.broadcasted_iota(jnp.int32, (HEAD_DIM, t), 0)

    for c in range(n_blocks):
        rows = pl.ds(c * t, t)
        kaug_ref[c, :, 0:HEAD_DIM] = k_ref[rows, :]
        kaug_ref[c, :, HEAD_DIM:2 * HEAD_DIM] = pos_ref[rows, :]
        vt_ref[c, 0:HEAD_DIM, :] = v_ref[rows, :].astype(F32).T.astype(BF16)
        vt_ref[c, HEAD_DIM:HEAD_DIM + ATT_ONES, :] = jnp.ones((ATT_ONES, t), BF16)
    for c in range(t // ATT_NQ):
        kk = lax.broadcasted_iota(jnp.int32, (t, ATT_NQ), 0)
        tt = lax.broadcasted_iota(jnp.int32, (t, ATT_NQ), 1) + c * ATT_NQ
        allowed = (kk // CHUNK) <= (tt // CHUNK)
        future = jnp.where(kk > tt, (2.0 * (c_hi + c_lo)) * (tt - kk).astype(F32), 0.0)
        diag_ref[c] = jnp.where(allowed, future, -jnp.inf)
    coef = jnp.where(row == 0, c_hi * 256.0,
                     jnp.where(row == 1, c_hi,
                               jnp.where(row == 2, c_lo * 256.0,
                                         jnp.where(row == 3, c_lo, 0.0)))).astype(BF16)
    qa1_ref[HEAD_DIM:2 * HEAD_DIM, :] = coef
    qa2_ref[HEAD_DIM:2 * HEAD_DIM, :] = coef

    def block_rows(qi):
        return pl.ds(pl.multiple_of(qi * t, t), t)

    def prep_q(qi):
        qt = (q_ref[block_rows(qi), :].astype(F32) * (QK_SCALE * LOG2E)).T
        qa1_ref[0:HEAD_DIM, :] = jnp.where(row < QK_HALF, qt, 0.0).astype(BF16)
        qa2_ref[0:HEAD_DIM, :] = jnp.where(row >= QK_HALF, qt, 0.0).astype(BF16)

    def init_stats():
        for m_ref, acc_ref in ((m1_ref, acc1_ref), (m2_ref, acc2_ref)):
            m_ref[...] = jnp.full((1, t), -jnp.inf, F32)
            acc_ref[...] = jnp.zeros((HEAD_DIM + ATT_ONES, t), F32)

    units = [(mp, c) for c in range(t // ATT_NQ) for mp in range(2)]
    qa_refs = (qa1_ref, qa2_ref)
    acc_refs = (acc1_ref, acc2_ref)
    m_refs = (m1_ref, m2_ref)

    def cols(c):
        return slice(c * ATT_NQ, (c + 1) * ATT_NQ)

    def scores(n, s_ref, mp, c):
        kb = kaug_ref[tab_ref[1, n]]
        s_ref[mp, c] = jnp.dot(kb, qa_refs[mp][:, cols(c)], preferred_element_type=F32)

    def softmax(s_ref, mp, c, diagonal):
        rows = slice(0, ATT_NQ * (c + 1)) if diagonal else slice(None)

        def biased():
            s = s_ref[mp, c, rows, :]
            return s + diag_ref[c, rows, :] if diagonal else s

        m_old = m_refs[mp][:, cols(c)]
        m_new = jnp.maximum(m_old, jnp.max(biased(), axis=0, keepdims=True))
        m_refs[mp][:, cols(c)] = m_new
        return jnp.exp2(m_old - m_new), jnp.exp2(biased() - m_new).astype(BF16)

    def weighted_values(n, mp, c, alpha, p):
        vt = vt_ref[tab_ref[1, n], :, 0:p.shape[0]]
        acc_refs[mp][:, cols(c)] = (alpha * acc_refs[mp][:, cols(c)]
                                    + jnp.dot(vt, p, preferred_element_type=F32))

    def compute(n, s_ref):
        for mp, c in units:
            scores(n, s_ref, mp, c)

    def consume(n, s_cur, s_next, diagonal):
        pending = None
        for mp, c in units:
            if s_next is not None:
                scores(n + 1, s_next, mp, c)
            ap = softmax(s_cur, mp, c, diagonal)
            if pending is not None:
                weighted_values(n, *pending)
            pending = (mp, c) + ap
        weighted_values(n, *pending)

    def finalize(qi):
        o1 = acc1_ref[0:HEAD_DIM, :] / acc1_ref[HEAD_DIM:HEAD_DIM + 1, :]
        o2 = acc2_ref[0:HEAD_DIM, :] / acc2_ref[HEAD_DIM:HEAD_DIM + 1, :]
        o = o1 - lam * o2
        o = o * lax.rsqrt(jnp.mean(o * o, axis=0, keepdims=True) + RMS_EPS)
        o_ref[block_rows(qi), :] = (o.T * gain_ref[...] * out_scale).astype(o_ref.dtype)

    def consume_step(n, s_cur, s_next):
        @pl.when(tab_ref[2, n] == 1)
        def _():
            init_stats()
            consume(n, s_cur, s_next, True)

        @pl.when(tab_ref[2, n] != 1)
        def _():
            consume(n, s_cur, s_next, False)

    def pipelined_step(n, s_cur, s_next):
        next_first = tab_ref[2, n + 1] == 1

        @pl.when(next_first)
        def _():
            prep_q(tab_ref[0, n + 1])

        consume_step(n, s_cur, s_next)

        @pl.when(next_first)
        def _():
            finalize(tab_ref[0, n])

    prep_q(0)
    compute(0, sa_ref)

    def body(m, carry):
        pipelined_step(2 * m, sa_ref, sb_ref)
        pipelined_step(2 * m + 1, sb_ref, sa_ref)
        return carry

    n_pairs = (n_steps - 1) // 2
    lax.fori_loop(0, n_pairs, body, 0)
    s_last = sa_ref
    if (n_steps - 1) % 2 == 1:
        pipelined_step(n_steps - 2, sa_ref, sb_ref)
        s_last = sb_ref
    consume_step(n_steps - 1, s_last, None)
    finalize(n_blocks - 1)


def _attention(p3, scalars, posfeat, gain, lam_init):
    b, s, _ = p3.shape
    t = ATT_T
    nb = s // t
    tables = _attn_tables(nb)
    kern = functools.partial(_attn_kernel, n_blocks=nb, n_steps=tables.shape[1], out_scale=1.0 - lam_init)
    hb = D_DIFF // HEAD_DIM
    return pl.pallas_call(
        kern,
        grid=(b, N_HEADS),
        in_specs=[
            pl.BlockSpec(memory_space=pltpu.SMEM),
            pl.BlockSpec(memory_space=pltpu.SMEM),
            pl.BlockSpec((None, s, HEAD_DIM), lambda b_, h: (b_, 0, h)),
            pl.BlockSpec((None, s, HEAD_DIM), lambda b_, h: (b_, 0, hb + h)),
            pl.BlockSpec((None, s, HEAD_DIM), lambda b_, h: (b_, 0, 2 * hb + h)),
            pl.BlockSpec((s, HEAD_DIM), lambda b_, h: (0, 0)),
            pl.BlockSpec((None, 1, HEAD_DIM), lambda b_, h: (h, 0, 0)),
        ],
        out_specs=pl.BlockSpec((None, s, HEAD_DIM), lambda b_, h: (b_, 0, h)),
        out_shape=jax.ShapeDtypeStruct((b, s, D_DIFF), BF16),
        scratch_shapes=[
            pltpu.VMEM((nb, t, 2 * HEAD_DIM), BF16),
            pltpu.VMEM((nb, HEAD_DIM + ATT_ONES, t), BF16),
            pltpu.VMEM((t // ATT_NQ, t, ATT_NQ), F32),
            pltpu.VMEM((2 * HEAD_DIM, t), BF16),
            pltpu.VMEM((2 * HEAD_DIM, t), BF16),
            pltpu.VMEM((2, t // ATT_NQ, t, ATT_NQ), F32),
            pltpu.VMEM((2, t // ATT_NQ, t, ATT_NQ), F32),
            pltpu.VMEM((1, t), F32), pltpu.VMEM((HEAD_DIM + ATT_ONES, t), F32),
            pltpu.VMEM((1, t), F32), pltpu.VMEM((HEAD_DIM + ATT_ONES, t), F32),
        ],
        compiler_params=_params("parallel", "parallel"),
        name="diff_attention",
    )(scalars, jnp.asarray(tables), p3, p3, p3, posfeat, gain)


def _mixer_kernel(a_ref, c1_ref, c2_ref, pw_ref, ps_ref, dww_ref, dwb_ref, lng_ref, lnb_ref,
                  cpw_ref, cpb_ref, o_ref, aext_ref, gext_ref, conv_ref):
    ts = MIX_TS
    hl = MIX_HALO
    si = pl.program_id(1)

    @pl.when(si == 0)
    def _():
        aext_ref[0:hl, :] = jnp.zeros((hl, D_POOL), F32)
        gext_ref[0:hl, :] = jnp.zeros((hl, CONV_CH), F32)

    @pl.when(si != 0)
    def _():
        aext_ref[0:hl, :] = aext_ref[ts:ts + hl, :]
        gext_ref[0:hl, :] = gext_ref[ts:ts + hl, :]

    aext_ref[hl:hl + ts, :] = a_ref[...].astype(F32)
    half_c1 = 0.5 * c1_ref[...].astype(F32)
    gext_ref[hl:hl + ts, :] = half_c1 + half_c1 * jnp.tanh(0.5 * c2_ref[...].astype(F32))

    tpos = si * ts + lax.broadcasted_iota(jnp.int32, (ts, 1), 0)
    for g, w in enumerate(POOL_WINDOWS):
        cols = slice(g * POOL_GROUP, (g + 1) * POOL_GROUP)
        x = aext_ref[:, cols]
        acc, span = x, 1
        while span < w:
            acc = acc + pltpu.roll(acc, span, 0)
            span *= 2
        cnt = jnp.minimum(tpos + 1, w).astype(F32)
        pooled = acc[hl:, :] / cnt - x[hl:, :]
        y = jnp.dot(pooled.astype(BF16), pw_ref[g], preferred_element_type=F32)
        o_ref[:, cols] = (y * ps_ref[:, cols]).astype(o_ref.dtype)

    rc = MIX_RC
    for r0 in range(0, ts, rc):
        for c0 in range(0, CONV_CH, 128):
            cols = slice(c0, c0 + 128)
            acc = jnp.zeros((rc, 128), F32) + dwb_ref[:, cols]
            for b in range(SUBLANES):
                part = None
                for a8 in range((CONV_K - 1 - b) // SUBLANES + 1):
                    d = SUBLANES * a8 + b
                    lo = hl + r0 - SUBLANES * a8 - (SUBLANES if b else 0)
                    win = gext_ref[lo:hl + r0 - SUBLANES * a8 + rc, cols]
                    term = dww_ref[CONV_K - 1 - d:CONV_K - d, cols] * win
                    part = term if part is None else part + term
                acc = acc + (pltpu.roll(part, b, 0)[SUBLANES:, :] if b else part)
            conv_ref[r0:r0 + rc, cols] = acc

    u = conv_ref[...]
    mu = jnp.mean(u, axis=-1, keepdims=True)
    var = jnp.mean(jnp.square(u - mu), axis=-1, keepdims=True)
    hz = (u - mu) * lax.rsqrt(var + LN_EPS) * lng_ref[...] + lnb_ref[...]
    z = hz + hz * jnp.tanh(hz)
    y = jnp.dot(z.astype(BF16), cpw_ref[...], preferred_element_type=F32) + cpb_ref[...]
    o_ref[:, D_POOL:D_POOL + CONV_CH] = y.astype(o_ref.dtype)


def _mixer(p3, pool_w, pool_scale, dw_w, dw_b, ln_g, ln_b, pw_w, pw_b):
    b, s, _ = p3.shape
    ts = MIX_TS
    a_blk = (3 * D_DIFF) // D_POOL
    full = lambda shape: pl.BlockSpec(shape, lambda b_, i: (0,) * len(shape))
    return pl.pallas_call(
        _mixer_kernel,
        grid=(b, s // ts),
        in_specs=[
            pl.BlockSpec((None, ts, D_POOL), lambda b_, i: (b_, i, a_blk)),
            pl.BlockSpec((None, ts, CONV_CH), lambda b_, i: (b_, i, a_blk + 1)),
            pl.BlockSpec((None, ts, CONV_CH), lambda b_, i: (b_, i, a_blk + 2)),
            full((len(POOL_WINDOWS), POOL_GROUP, POOL_GROUP)),
            full((1, D_POOL)),
            full((CONV_K, CONV_CH)),
            full((1, CONV_CH)),
            full((1, CONV_CH)),
            full((1, CONV_CH)),
            full((CONV_CH, CONV_CH)),
            full((1, CONV_CH)),
        ],
        out_specs=pl.BlockSpec((None, ts, D_POOL + CONV_CH), lambda b_, i: (b_, i, 0)),
        out_shape=jax.ShapeDtypeStruct((b, s, D_POOL + CONV_CH), BF16),
        scratch_shapes=[
            pltpu.VMEM((MIX_HALO + ts, D_POOL), F32),
            pltpu.VMEM((MIX_HALO + ts, CONV_CH), F32),
            pltpu.VMEM((ts, CONV_CH), F32),
        ],
        compiler_params=_params("parallel", "arbitrary"),
        name="pool_conformer",
    )(p3, p3, p3, pool_w, pool_scale, dw_w, dw_b, ln_g, ln_b, pw_w, pw_b)


def _out_proj_kernel(x_ref, yb_ref, yac_ref, wb_ref, wac_ref, o_ref):
    acc = jnp.dot(yb_ref[...], wb_ref[...], preferred_element_type=F32)
    acc = acc + jnp.dot(yac_ref[...], wac_ref[...], preferred_element_type=F32)
    o_ref[...] = x_ref[...] + acc


def _out_proj(x2, yb, yac, w):
    m = x2.shape[0]
    tm = min(OUT_TM, m)
    tn = OUT_TN
    kb = D_DIFF
    return pl.pallas_call(
        _out_proj_kernel,
        grid=(m // tm, D_MODEL // tn),
        in_specs=[
            pl.BlockSpec((tm, tn), lambda i, j: (i, j)),
            pl.BlockSpec((tm, kb), lambda i, j: (i, 0)),
            pl.BlockSpec((tm, D_MIX - kb), lambda i, j: (i, 0)),
            pl.BlockSpec((kb, tn), lambda i, j: (0, j)),
            pl.BlockSpec((D_MIX - kb, tn), lambda i, j: (1, j)),
        ],
        out_specs=pl.BlockSpec((tm, tn), lambda i, j: (i, j)),
        out_shape=jax.ShapeDtypeStruct((m, D_MODEL), F32),
        compiler_params=_params("parallel", "arbitrary"),
        name="out_proj",
    )(x2, yb, yac, w, w)


def _ffn_kernel(*refs, tiles_per_seq, n_tiles, final_norm):
    x_ref, g_ref, wg_ref, wu_ref, cw_ref, cb_ref, wo_ref = refs[:7]
    fg_ref = refs[7] if final_norm else None
    o_ref, h_ref, gext_ref, halo_ref = refs[-4:]
    tm = FFN_TM
    mi = pl.program_id(0)
    f = pl.program_id(1)

    @pl.when(f == 0)
    def _():
        x = x_ref[...]
        h_ref[...] = _rms(x, g_ref[...]).astype(BF16)
        o_ref[...] = x

    first = (mi % tiles_per_seq) == 0

    @pl.when(first)
    def _():
        gext_ref[:, 0:SUBLANES, :] = jnp.zeros((FFN_TF // FFN_TC, SUBLANES, FFN_TC), F32)

    @pl.when(jnp.logical_not(first))
    def _():
        gext_ref[:, 0:SUBLANES, :] = halo_ref[f]

    h = h_ref[...]
    out = None
    for c in range(FFN_TF // FFN_TC):
        cols = slice(c * FFN_TC, (c + 1) * FFN_TC)
        gate = jnp.dot(h, wg_ref[:, cols], preferred_element_type=F32)
        up = jnp.dot(h, wu_ref[:, cols], preferred_element_type=F32)
        gext_ref[c, SUBLANES:SUBLANES + tm, :] = gate
        halo_ref[f, c] = gate[tm - SUBLANES:, :]
        gx = gext_ref[c]
        g1 = pltpu.roll(gx, 1, 0)[SUBLANES:, :]
        g2 = pltpu.roll(gx, 2, 0)[SUBLANES:, :]
        hg = cw_ref[0:1, cols] * g2 + cw_ref[1:2, cols] * g1 + cw_ref[2:3, cols] * gate + cb_ref[:, cols]
        act = ((hg + hg * jnp.tanh(hg)) * up).astype(BF16)
        part = jnp.dot(act, wo_ref[cols, :], preferred_element_type=F32)
        out = part if out is None else out + part
    o_ref[...] += out

    if final_norm:
        @pl.when(f == n_tiles - 1)
        def _():
            o_ref[...] = _rms(o_ref[...], fg_ref[...])


def _ffn(x2, g, wg, wu, cw, cb, wo, seq, final_gain=None):
    m = x2.shape[0]
    tm, tf = FFN_TM, FFN_TF
    nf = D_FF_PAD // tf
    kern = functools.partial(_ffn_kernel, tiles_per_seq=seq // tm, n_tiles=nf, final_norm=final_gain is not None)
    in_specs = [
        pl.BlockSpec((tm, D_MODEL), lambda i, j: (i, 0)),
        pl.BlockSpec((1, D_MODEL), lambda i, j: (0, 0)),
        pl.BlockSpec((D_MODEL, tf), lambda i, j: (0, j)),
        pl.BlockSpec((D_MODEL, tf), lambda i, j: (0, j)),
        pl.BlockSpec((FFN_CONV_K, tf), lambda i, j: (0, j)),
        pl.BlockSpec((1, tf), lambda i, j: (0, j)),
        pl.BlockSpec((tf, D_MODEL), lambda i, j: (j, 0)),
    ]
    args = [x2, g, wg, wu, cw, cb, wo]
    if final_gain is not None:
        in_specs.append(pl.BlockSpec((1, D_MODEL), lambda i, j: (0, 0)))
        args.append(final_gain)
    return pl.pallas_call(
        kern,
        grid=(m // tm, nf),
        in_specs=in_specs,
        out_specs=pl.BlockSpec((tm, D_MODEL), lambda i, j: (i, 0)),
        out_shape=jax.ShapeDtypeStruct((m, D_MODEL), F32),
        scratch_shapes=[
            pltpu.VMEM((tm, D_MODEL), BF16),
            pltpu.VMEM((tf // FFN_TC, SUBLANES + tm, FFN_TC), F32),
            pltpu.VMEM((nf, tf // FFN_TC, SUBLANES, FFN_TC), F32),
        ],
        compiler_params=_params("arbitrary", "arbitrary"),
        name="convglu_ffn",
    )(*args)


def _pad_ff(a, axis):
    pad = [(0, 0)] * a.ndim
    pad[axis] = (0, D_FF_PAD - D_FF)
    return jnp.pad(a, pad)


def kernel(x, norm_mix, w_in, lam_q1, lam_k1, lam_q2, lam_k2, diff_head_gain, pool_w, pool_scale,
           conv_dw_w, conv_dw_b, conv_ln_g, conv_ln_b, conv_pw_w, conv_pw_b, w_out, norm_ffn,
           w_ffn_in, ffn_conv_w, ffn_conv_b, w_ffn_out, final_norm):
    bsz, seq, _ = x.shape
    m = bsz * seq
    assert seq % MIX_TS == 0 and seq % ATT_T == 0 and seq % FFN_TM == 0 and seq <= 256 * 256

    coef = np.asarray([2.0 ** (-8.0 * (h + 1.0) / N_HEADS) * LOG2E for h in range(N_HEADS)], np.float64)
    coef_hi = coef.astype(BF16).astype(np.float64)
    coef_lo = (coef - coef_hi).astype(BF16).astype(np.float64)
    coefs = jnp.asarray(np.concatenate([coef_hi, coef_lo]), F32)
    kpos = np.arange(seq)
    posfeat = np.zeros((seq, HEAD_DIM), np.float32)
    posfeat[:, 0] = posfeat[:, 2] = kpos // 256
    posfeat[:, 1] = posfeat[:, 3] = kpos % 256
    posfeat = jnp.asarray(posfeat, BF16)

    x2 = x.reshape(m, D_MODEL)
    for l in range(DEPTH):
        lam_init = 0.8 - 0.6 * math.exp(-0.3 * l)
        lam = (jnp.exp(jnp.sum(lam_q1[l] * lam_k1[l])) - jnp.exp(jnp.sum(lam_q2[l] * lam_k2[l])) + lam_init)
        scalars = jnp.concatenate([lam.reshape(1).astype(F32), coefs])

        p = _in_proj(x2, norm_mix[l].reshape(1, D_MODEL), w_in[l].astype(BF16))
        p3 = p.reshape(bsz, seq, D_IN)
        y_b = _attention(p3, scalars, posfeat, diff_head_gain[l].reshape(N_HEADS, 1, HEAD_DIM), lam_init)
        y_ac = _mixer(p3, pool_w[l].astype(BF16), pool_scale[l].reshape(1, D_POOL), conv_dw_w[l],
                      conv_dw_b[l].reshape(1, CONV_CH), 0.5 * conv_ln_g[l].reshape(1, CONV_CH),
                      0.5 * conv_ln_b[l].reshape(1, CONV_CH), conv_pw_w[l].astype(BF16),
                      conv_pw_b[l].reshape(1, CONV_CH))
        x2 = _out_proj(x2, y_b.reshape(m, D_DIFF), y_ac.reshape(m, D_POOL + CONV_CH), w_out[l].astype(BF16))

        wg = _pad_ff(w_ffn_in[l][:, :D_FF].astype(BF16), 1)
        wu = _pad_ff(w_ffn_in[l][:, D_FF:].astype(BF16), 1)
        cw = _pad_ff(0.5 * ffn_conv_w[l], 1)
        cb = _pad_ff(0.5 * ffn_conv_b[l].reshape(1, D_FF), 1)
        wo = _pad_ff(w_ffn_out[l].astype(BF16), 0)
        final_gain = final_norm.reshape(1, D_MODEL) if l == DEPTH - 1 else None
        x2 = _ffn(x2, norm_ffn[l].reshape(1, D_MODEL), wg, wu, cw, cb, wo, seq, final_gain)

    return x2.reshape(bsz, seq, D_MODEL)
```

```python
import functools
import math

import jax
import jax.numpy as jnp
import numpy as np
from jax import lax
from jax.experimental import pallas as pl
from jax.experimental.pallas import tpu as pltpu

F32 = jnp.float32
BF16 = jnp.bfloat16

D_MODEL = 2048
DEPTH = 4
CHUNK = 64
D_DIFF = 1024
HEAD_DIM = 128
QK_HALF = 64
N_HEADS = 8
QK_SCALE = 1.0 / math.sqrt(QK_HALF)
LOG2E = 1.4426950408889634
D_POOL = 512
POOL_WINDOWS = (2, 4, 8, 16)
POOL_GROUP = 128
CONV_CH = 512
CONV_K = 31
D_IN = 3 * D_DIFF + D_POOL + 2 * CONV_CH
D_MIX = D_DIFF + D_POOL + CONV_CH
D_FF = 5504
FFN_CONV_K = 3
RMS_EPS = 1e-6
LN_EPS = 1e-5

VMEM_LIMIT_BYTES = 56 * 1024 * 1024
SUBLANES = 8

IN_TM, IN_TN = 1024, 1536
ATT_T = 512
ATT_NQ = 256
ATT_ONES = 16
MIX_TS = 512
MIX_HALO = 32
MIX_RC = 64
OUT_TM, OUT_TN = 512, 2048
FFN_TM, FFN_TF = 512, 512
FFN_TC = 256
D_FF_PAD = ((D_FF + FFN_TF - 1) // FFN_TF) * FFN_TF


def _params(*sem):
    return pltpu.CompilerParams(dimension_semantics=sem, vmem_limit_bytes=VMEM_LIMIT_BYTES)


def _rms(x, g):
    return x * lax.rsqrt(jnp.mean(x * x, axis=-1, keepdims=True) + RMS_EPS) * g


def _in_proj_kernel(x_ref, g_ref, w_ref, o_ref, h_ref):
    @pl.when(pl.program_id(1) == 0)
    def _():
        h_ref[...] = _rms(x_ref[...], g_ref[...]).astype(BF16)

    o_ref[...] = jnp.dot(h_ref[...], w_ref[...], preferred_element_type=F32).astype(o_ref.dtype)


def _in_proj(x2, g, w_all, layer):
    m = x2.shape[0]
    tm = min(IN_TM, m)
    return pl.pallas_call(
        _in_proj_kernel,
        grid=(m // tm, D_IN // IN_TN),
        in_specs=[
            pl.BlockSpec((tm, D_MODEL), lambda i, j: (i, 0)),
            pl.BlockSpec((1, D_MODEL), lambda i, j: (0, 0)),
            pl.BlockSpec((None, D_MODEL, IN_TN), lambda i, j: (layer, 0, j)),
        ],
        out_specs=pl.BlockSpec((tm, IN_TN), lambda i, j: (i, j)),
        out_shape=jax.ShapeDtypeStruct((m, D_IN), BF16),
        scratch_shapes=[pltpu.VMEM((tm, D_MODEL), BF16)],
        compiler_params=_params("parallel", "arbitrary"),
        name="in_proj",
    )(x2, g, w_all)


def _attn_tables(nb):
    qi, kj, first = [], [], []
    for i in range(nb):
        for idx, j in enumerate([i] + list(range(i))):
            qi.append(i)
            kj.append(j)
            first.append(1 if idx == 0 else 0)
    return np.asarray([qi, kj, first], np.int32)


def _attn_kernel(sc_ref, tab_ref, q_ref, k_ref, v_ref, pos_ref, gain_ref, o_ref,
                 kaug_ref, vt_ref, diag_ref, qa1_ref, qa2_ref, sa_ref, sb_ref,
                 m1_ref, acc1_ref, m2_ref, acc2_ref, *, n_blocks, n_steps, out_scale):
    t = ATT_T
    h = pl.program_id(1)
    lam = sc_ref[0]
    c_hi = sc_ref[1 + h]
    c_lo = sc_ref[1 + N_HEADS + h]
    row = lax.broadcasted_iota(jnp.int32, (HEAD_DIM, t), 0)

    for c in range(n_blocks):
        rows = pl.ds(c * t, t)
        kaug_ref[c, :, 0:HEAD_DIM] = k_ref[rows, :]
        kaug_ref[c, :, HEAD_DIM:2 * HEAD_DIM] = pos_ref[rows, :]
        vt_ref[c, 0:HEAD_DIM, :] = v_ref[rows, :].astype(F32).T.astype(BF16)
        vt_ref[c, HEAD_DIM:HEAD_DIM + ATT_ONES, :] = jnp.ones((ATT_ONES, t), BF16)
    for c in range(t // ATT_NQ):
        kk = lax.broadcasted_iota(jnp.int32, (t, ATT_NQ), 0)
        tt = lax.broadcasted_iota(jnp.int32, (t, ATT_NQ), 1) + c * ATT_NQ
        allowed = (kk // CHUNK) <= (tt // CHUNK)
        future = jnp.where(kk > tt, (2.0 * (c_hi + c_lo)) * (tt - kk).astype(F32), 0.0)
        diag_ref[c] = jnp.where(allowed, future, -jnp.inf)
    coef = jnp.where(row == 0, c_hi * 256.0,
                     jnp.where(row == 1, c_hi,
                               jnp.where(row == 2, c_lo * 256.0,
                                         jnp.where(row == 3, c_lo, 0.0)))).astype(BF16)
    qa1_ref[HEAD_DIM:2 * HEAD_DIM, :] = coef
    qa2_ref[HEAD_DIM:2 * HEAD_DIM, :] = coef

    def block_rows(qi):
        return pl.ds(pl.multiple_of(qi * t, t), t)

    def prep_q(qi):
        qt = (q_ref[block_rows(qi), :].astype(F32) * (QK_SCALE * LOG2E)).T
        qa1_ref[0:HEAD_DIM, :] = jnp.where(row < QK_HALF, qt, 0.0).astype(BF16)
        qa2_ref[0:HEAD_DIM, :] = jnp.where(row >= QK_HALF, qt, 0.0).astype(BF16)

    def init_stats():
        for m_ref, acc_ref in ((m1_ref, acc1_ref), (m2_ref, acc2_ref)):
            m_ref[...] = jnp.full((1, t), -jnp.inf, F32)
            acc_ref[...] = jnp.zeros((HEAD_DIM + ATT_ONES, t), F32)

    units = [(mp, c) for c in range(t // ATT_NQ) for mp in range(2)]
    qa_refs = (qa1_ref, qa2_ref)
    acc_refs = (acc1_ref, acc2_ref)
    m_refs = (m1_ref, m2_ref)

    def cols(c):
        return slice(c * ATT_NQ, (c + 1) * ATT_NQ)

    def scores(n, s_ref, mp, c):
        kb = kaug_ref[tab_ref[1, n]]
        s_ref[mp, c] = jnp.dot(kb, qa_refs[mp][:, cols(c)], preferred_element_type=F32)

    def softmax(s_ref, mp, c, diagonal):
        rows = slice(0, ATT_NQ * (c + 1)) if diagonal else slice(None)

        def biased():
            s = s_ref[mp, c, rows, :]
            return s + diag_ref[c, rows, :] if diagonal else s

        m_old = m_refs[mp][:, cols(c)]
        m_new = jnp.maximum(m_old, jnp.max(biased(), axis=0, keepdims=True))
        m_refs[mp][:, cols(c)] = m_new
        return jnp.exp2(m_old - m_new), jnp.exp2(biased() - m_new).astype(BF16)

    def weighted_values(n, mp, c, alpha, p):
        vt = vt_ref[tab_ref[1, n], :, 0:p.shape[0]]
        acc_refs[mp][:, cols(c)] = (alpha * acc_refs[mp][:, cols(c)]
                                    + jnp.dot(vt, p, preferred_element_type=F32))

    def compute(n, s_ref):
        for mp, c in units:
            scores(n, s_ref, mp, c)

    def consume(n, s_cur, s_next, diagonal):
        pending = None
        for mp, c in units:
            if s_next is not None:
                scores(n + 1, s_next, mp, c)
            ap = softmax(s_cur, mp, c, diagonal)
            if pending is not None:
                weighted_values(n, *pending)
            pending = (mp, c) + ap
        weighted_values(n, *pending)

    def finalize(qi):
        r1 = 1.0 / acc1_ref[HEAD_DIM:HEAD_DIM + 1, :]
        r2 = lam / acc2_ref[HEAD_DIM:HEAD_DIM + 1, :]
        o = acc1_ref[0:HEAD_DIM, :] * r1 - acc2_ref[0:HEAD_DIM, :] * r2
        o = o * lax.rsqrt(jnp.mean(o * o, axis=0, keepdims=True) + RMS_EPS)
        o_ref[block_rows(qi), :] = (o.T * gain_ref[...] * out_scale).astype(o_ref.dtype)

    def consume_step(n, s_cur, s_next):
        @pl.when(tab_ref[2, n] == 1)
        def _():
            init_stats()
            consume(n, s_cur, s_next, True)

        @pl.when(tab_ref[2, n] != 1)
        def _():
            consume(n, s_cur, s_next, False)

    def pipelined_step(n, s_cur, s_next):
        next_first = tab_ref[2, n + 1] == 1

        @pl.when(next_first)
        def _():
            prep_q(tab_ref[0, n + 1])

        consume_step(n, s_cur, s_next)

        @pl.when(next_first)
        def _():
            finalize(tab_ref[0, n])

    prep_q(0)
    compute(0, sa_ref)

    def body(m, carry):
        pipelined_step(2 * m, sa_ref, sb_ref)
        pipelined_step(2 * m + 1, sb_ref, sa_ref)
        return carry

    n_pairs = (n_steps - 1) // 2
    lax.fori_loop(0, n_pairs, body, 0)
    s_last = sa_ref
    if (n_steps - 1) % 2 == 1:
        pipelined_step(n_steps - 2, sa_ref, sb_ref)
        s_last = sb_ref
    consume_step(n_steps - 1, s_last, None)
    finalize(n_blocks - 1)


def _attention(p3, scalars, posfeat, gain, lam_init):
    b, s, _ = p3.shape
    t = ATT_T
    nb = s // t
    tables = _attn_tables(nb)
    kern = functools.partial(_attn_kernel, n_blocks=nb, n_steps=tables.shape[1], out_scale=1.0 - lam_init)
    hb = D_DIFF // HEAD_DIM
    return pl.pallas_call(
        kern,
        grid=(b, N_HEADS),
        in_specs=[
            pl.BlockSpec(memory_space=pltpu.SMEM),
            pl.BlockSpec(memory_space=pltpu.SMEM),
            pl.BlockSpec((None, s, HEAD_DIM), lambda b_, h: (b_, 0, h)),
            pl.BlockSpec((None, s, HEAD_DIM), lambda b_, h: (b_, 0, hb + h)),
            pl.BlockSpec((None, s, HEAD_DIM), lambda b_, h: (b_, 0, 2 * hb + h)),
            pl.BlockSpec((s, HEAD_DIM), lambda b_, h: (0, 0)),
            pl.BlockSpec((None, 1, HEAD_DIM), lambda b_, h: (h, 0, 0)),
        ],
        out_specs=pl.BlockSpec((None, s, HEAD_DIM), lambda b_, h: (b_, 0, h)),
        out_shape=jax.ShapeDtypeStruct((b, s, D_DIFF), BF16),
        scratch_shapes=[
            pltpu.VMEM((nb, t, 2 * HEAD_DIM), BF16),
            pltpu.VMEM((nb, HEAD_DIM + ATT_ONES, t), BF16),
            pltpu.VMEM((t // ATT_NQ, t, ATT_NQ), F32),
            pltpu.VMEM((2 * HEAD_DIM, t), BF16),
            pltpu.VMEM((2 * HEAD_DIM, t), BF16),
            pltpu.VMEM((2, t // ATT_NQ, t, ATT_NQ), F32),
            pltpu.VMEM((2, t // ATT_NQ, t, ATT_NQ), F32),
            pltpu.VMEM((1, t), F32), pltpu.VMEM((HEAD_DIM + ATT_ONES, t), F32),
            pltpu.VMEM((1, t), F32), pltpu.VMEM((HEAD_DIM + ATT_ONES, t), F32),
        ],
        compiler_params=_params("parallel", "parallel"),
        name="diff_attention",
    )(scalars, jnp.asarray(tables), p3, p3, p3, posfeat, gain)


def _mixer_kernel(a_ref, c1_ref, c2_ref, pw_ref, ps_ref, dww_ref, dwb_ref, lng_ref, lnb_ref,
                  cpw_ref, cpb_ref, o_ref, aext_ref, gext_ref, conv_ref):
    ts = MIX_TS
    hl = MIX_HALO
    si = pl.program_id(1)

    @pl.when(si == 0)
    def _():
        aext_ref[0:hl, :] = jnp.zeros((hl, D_POOL), F32)
        gext_ref[0:hl, :] = jnp.zeros((hl, CONV_CH), F32)

    @pl.when(si != 0)
    def _():
        aext_ref[0:hl, :] = aext_ref[ts:ts + hl, :]
        gext_ref[0:hl, :] = gext_ref[ts:ts + hl, :]

    aext_ref[hl:hl + ts, :] = a_ref[...].astype(F32)
    half_c1 = 0.5 * c1_ref[...].astype(F32)
    gext_ref[hl:hl + ts, :] = half_c1 + half_c1 * jnp.tanh(0.5 * c2_ref[...].astype(F32))

    tpos = si * ts + lax.broadcasted_iota(jnp.int32, (ts, 1), 0)
    for g, w in enumerate(POOL_WINDOWS):
        cols = slice(g * POOL_GROUP, (g + 1) * POOL_GROUP)
        x = aext_ref[:, cols]
        acc, span = x, 1
        while span < w:
            acc = acc + pltpu.roll(acc, span, 0)
            span *= 2
        cnt = jnp.minimum(tpos + 1, w).astype(F32)
        pooled = acc[hl:, :] / cnt - x[hl:, :]
        y = jnp.dot(pooled.astype(BF16), pw_ref[g], preferred_element_type=F32)
        o_ref[:, cols] = (y * ps_ref[:, cols]).astype(o_ref.dtype)

    rc = MIX_RC
    for r0 in range(0, ts, rc):
        for c0 in range(0, CONV_CH, 128):
            cols = slice(c0, c0 + 128)
            acc = jnp.zeros((rc, 128), F32) + dwb_ref[:, cols]
            for b in range(SUBLANES):
                part = None
                for a8 in range((CONV_K - 1 - b) // SUBLANES + 1):
                    d = SUBLANES * a8 + b
                    lo = hl + r0 - SUBLANES * a8 - (SUBLANES if b else 0)
                    win = gext_ref[lo:hl + r0 - SUBLANES * a8 + rc, cols]
                    term = dww_ref[CONV_K - 1 - d:CONV_K - d, cols] * win
                    part = term if part is None else part + term
                acc = acc + (pltpu.roll(part, b, 0)[SUBLANES:, :] if b else part)
            conv_ref[r0:r0 + rc, cols] = acc

    u = conv_ref[...]
    mu = jnp.mean(u, axis=-1, keepdims=True)
    var = jnp.mean(jnp.square(u - mu), axis=-1, keepdims=True)
    hz = (u - mu) * lax.rsqrt(var + LN_EPS) * lng_ref[...] + lnb_ref[...]
    z = hz + hz * jnp.tanh(hz)
    y = jnp.dot(z.astype(BF16), cpw_ref[...], preferred_element_type=F32) + cpb_ref[...]
    o_ref[:, D_POOL:D_POOL + CONV_CH] = y.astype(o_ref.dtype)


def _mixer(p3, pool_w, pool_scale, dw_w, dw_b, ln_g, ln_b, pw_w, pw_b):
    b, s, _ = p3.shape
    ts = MIX_TS
    a_blk = (3 * D_DIFF) // D_POOL
    full = lambda shape: pl.BlockSpec(shape, lambda b_, i: (0,) * len(shape))
    return pl.pallas_call(
        _mixer_kernel,
        grid=(b, s // ts),
        in_specs=[
            pl.BlockSpec((None, ts, D_POOL), lambda b_, i: (b_, i, a_blk)),
            pl.BlockSpec((None, ts, CONV_CH), lambda b_, i: (b_, i, a_blk + 1)),
            pl.BlockSpec((None, ts, CONV_CH), lambda b_, i: (b_, i, a_blk + 2)),
            full((len(POOL_WINDOWS), POOL_GROUP, POOL_GROUP)),
            full((1, D_POOL)),
            full((CONV_K, CONV_CH)),
            full((1, CONV_CH)),
            full((1, CONV_CH)),
            full((1, CONV_CH)),
            full((CONV_CH, CONV_CH)),
            full((1, CONV_CH)),
        ],
        out_specs=pl.BlockSpec((None, ts, D_POOL + CONV_CH), lambda b_, i: (b_, i, 0)),
        out_shape=jax.ShapeDtypeStruct((b, s, D_POOL + CONV_CH), BF16),
        scratch_shapes=[
            pltpu.VMEM((MIX_HALO + ts, D_POOL), F32),
            pltpu.VMEM((MIX_HALO + ts, CONV_CH), F32),
            pltpu.VMEM((ts, CONV_CH), F32),
        ],
        compiler_params=_params("parallel", "arbitrary"),
        name="pool_conformer",
    )(p3, p3, p3, pool_w, pool_scale, dw_w, dw_b, ln_g, ln_b, pw_w, pw_b)


def _out_proj_kernel(x_ref, yb_ref, yac_ref, wb_ref, wac_ref, o_ref):
    acc = jnp.dot(yb_ref[...], wb_ref[...], preferred_element_type=F32)
    acc = acc + jnp.dot(yac_ref[...], wac_ref[...], preferred_element_type=F32)
    o_ref[...] = x_ref[...] + acc


def _out_proj(x2, yb, yac, w_all, layer):
    m = x2.shape[0]
    tm = min(OUT_TM, m)
    tn = OUT_TN
    kb = D_DIFF
    return pl.pallas_call(
        _out_proj_kernel,
        grid=(m // tm, D_MODEL // tn),
        in_specs=[
            pl.BlockSpec((tm, tn), lambda i, j: (i, j)),
            pl.BlockSpec((tm, kb), lambda i, j: (i, 0)),
            pl.BlockSpec((tm, D_MIX - kb), lambda i, j: (i, 0)),
            pl.BlockSpec((None, kb, tn), lambda i, j: (layer, 0, j)),
            pl.BlockSpec((None, D_MIX - kb, tn), lambda i, j: (layer, 1, j)),
        ],
        out_specs=pl.BlockSpec((tm, tn), lambda i, j: (i, j)),
        out_shape=jax.ShapeDtypeStruct((m, D_MODEL), F32),
        compiler_params=_params("parallel", "arbitrary"),
        name="out_proj",
    )(x2, yb, yac, w_all, w_all)


def _ffn_kernel(*refs, tiles_per_seq, n_tiles, final_norm):
    x_ref, g_ref, wg_ref, wu_ref, cw_ref, cb_ref, wo_ref = refs[:7]
    fg_ref = refs[7] if final_norm else None
    o_ref, h_ref, gext_ref, halo_ref = refs[-4:]
    tm = FFN_TM
    mi = pl.program_id(0)
    f = pl.program_id(1)

    @pl.when(f == 0)
    def _():
        x = x_ref[...]
        h_ref[...] = _rms(x, g_ref[...]).astype(BF16)
        o_ref[...] = x

    first = (mi % tiles_per_seq) == 0

    @pl.when(first)
    def _():
        gext_ref[:, 0:SUBLANES, :] = jnp.zeros((FFN_TF // FFN_TC, SUBLANES, FFN_TC), F32)

    @pl.when(jnp.logical_not(first))
    def _():
        gext_ref[:, 0:SUBLANES, :] = halo_ref[f]

    h = h_ref[...]
    out = None
    for c in range(FFN_TF // FFN_TC):
        cols = slice(c * FFN_TC, (c + 1) * FFN_TC)
        gate = jnp.dot(h, wg_ref[:, cols], preferred_element_type=F32)
        up = jnp.dot(h, wu_ref[:, cols], preferred_element_type=F32)
        gext_ref[c, SUBLANES:SUBLANES + tm, :] = gate
        halo_ref[f, c] = gate[tm - SUBLANES:, :]
        gx = gext_ref[c]
        g1 = pltpu.roll(gx, 1, 0)[SUBLANES:, :]
        g2 = pltpu.roll(gx, 2, 0)[SUBLANES:, :]
        hg = cw_ref[0:1, cols] * g2 + cw_ref[1:2, cols] * g1 + cw_ref[2:3, cols] * gate + cb_ref[:, cols]
        act = ((hg + hg * jnp.tanh(hg)) * up).astype(BF16)
        part = jnp.dot(act, wo_ref[cols, :], preferred_element_type=F32)
        out = part if out is None else out + part
    o_ref[...] += out

    if final_norm:
        @pl.when(f == n_tiles - 1)
        def _():
            o_ref[...] = _rms(o_ref[...], fg_ref[...])


def _ffn(x2, g, w_in_all, cw_all, cb_all, w_out_all, layer, seq, final_gain=None):
    m = x2.shape[0]
    tm, tf = FFN_TM, FFN_TF
    nf = D_FF_PAD // tf
    kern = functools.partial(_ffn_kernel, tiles_per_seq=seq // tm, n_tiles=nf, final_norm=final_gain is not None)
    in_specs = [
        pl.BlockSpec((tm, D_MODEL), lambda i, j: (i, 0)),
        pl.BlockSpec((1, D_MODEL), lambda i, j: (0, 0)),
        pl.BlockSpec((None, D_MODEL, tf), lambda i, j: (layer, 0, j)),
        pl.BlockSpec((None, D_MODEL, tf), lambda i, j: (layer, 0, nf + j)),
        pl.BlockSpec((None, FFN_CONV_K, tf), lambda i, j: (layer, 0, j)),
        pl.BlockSpec((None, 1, tf), lambda i, j: (layer, 0, j)),
        pl.BlockSpec((None, tf, D_MODEL), lambda i, j: (layer, j, 0)),
    ]
    args = [x2, g, w_in_all, w_in_all, cw_all, cb_all, w_out_all]
    if final_gain is not None:
        in_specs.append(pl.BlockSpec((1, D_MODEL), lambda i, j: (0, 0)))
        args.append(final_gain)
    return pl.pallas_call(
        kern,
        grid=(m // tm, nf),
        in_specs=in_specs,
        out_specs=pl.BlockSpec((tm, D_MODEL), lambda i, j: (i, 0)),
        out_shape=jax.ShapeDtypeStruct((m, D_MODEL), F32),
        scratch_shapes=[
            pltpu.VMEM((tm, D_MODEL), BF16),
            pltpu.VMEM((tf // FFN_TC, SUBLANES + tm, FFN_TC), F32),
            pltpu.VMEM((nf, tf // FFN_TC, SUBLANES, FFN_TC), F32),
        ],
        compiler_params=_params("arbitrary", "arbitrary"),
        name="convglu_ffn",
    )(*args)


def kernel(x, norm_mix, w_in, lam_q1, lam_k1, lam_q2, lam_k2, diff_head_gain, pool_w, pool_scale,
           conv_dw_w, conv_dw_b, conv_ln_g, conv_ln_b, conv_pw_w, conv_pw_b, w_out, norm_ffn,
           w_ffn_in, ffn_conv_w, ffn_conv_b, w_ffn_out, final_norm):
    bsz, seq, _ = x.shape
    m = bsz * seq
    assert seq % MIX_TS == 0 and seq % ATT_T == 0 and seq % FFN_TM == 0 and seq <= 256 * 256

    coef = np.asarray([2.0 ** (-8.0 * (h + 1.0) / N_HEADS) * LOG2E for h in range(N_HEADS)], np.float64)
    coef_hi = coef.astype(BF16).astype(np.float64)
    coef_lo = (coef - coef_hi).astype(BF16).astype(np.float64)
    coefs = jnp.asarray(np.concatenate([coef_hi, coef_lo]), F32)
    kpos = np.arange(seq)
    posfeat = np.zeros((seq, HEAD_DIM), np.float32)
    posfeat[:, 0] = posfeat[:, 2] = kpos // 256
    posfeat[:, 1] = posfeat[:, 3] = kpos % 256
    posfeat = jnp.asarray(posfeat, BF16)

    ff_pad = D_FF_PAD - D_FF
    w_in_b = w_in.astype(BF16)
    w_out_b = w_out.astype(BF16)
    w_ffn_in_b = jnp.pad(w_ffn_in.astype(BF16).reshape(DEPTH, D_MODEL, 2, D_FF),
                         ((0, 0), (0, 0), (0, 0), (0, ff_pad))).reshape(DEPTH, D_MODEL, 2 * D_FF_PAD)
    w_ffn_out_b = jnp.pad(w_ffn_out.astype(BF16), ((0, 0), (0, ff_pad), (0, 0)))
    cw_all = jnp.pad(0.5 * ffn_conv_w, ((0, 0), (0, 0), (0, ff_pad)))
    cb_all = jnp.pad(0.5 * ffn_conv_b, ((0, 0), (0, ff_pad))).reshape(DEPTH, 1, D_FF_PAD)

    x2 = x.reshape(m, D_MODEL)
    for l in range(DEPTH):
        lam_init = 0.8 - 0.6 * math.exp(-0.3 * l)
        lam = (jnp.exp(jnp.sum(lam_q1[l] * lam_k1[l])) - jnp.exp(jnp.sum(lam_q2[l] * lam_k2[l])) + lam_init)
        scalars = jnp.concatenate([lam.reshape(1).astype(F32), coefs])

        p = _in_proj(x2, norm_mix[l].reshape(1, D_MODEL), w_in_b, l)
        p3 = p.reshape(bsz, seq, D_IN)
        y_b = _attention(p3, scalars, posfeat, diff_head_gain[l].reshape(N_HEADS, 1, HEAD_DIM), lam_init)
        y_ac = _mixer(p3, pool_w[l].astype(BF16), pool_scale[l].reshape(1, D_POOL), conv_dw_w[l],
                      conv_dw_b[l].reshape(1, CONV_CH), 0.5 * conv_ln_g[l].reshape(1, CONV_CH),
                      0.5 * conv_ln_b[l].reshape(1, CONV_CH), conv_pw_w[l].astype(BF16),
                      conv_pw_b[l].reshape(1, CONV_CH))
        x2 = _out_proj(x2, y_b.reshape(m, D_DIFF), y_ac.reshape(m, D_POOL + CONV_CH), w_out_b, l)

        final_gain = final_norm.reshape(1, D_MODEL) if l == DEPTH - 1 else None
        x2 = _ffn(x2, norm_ffn[l].reshape(1, D_MODEL), w_ffn_in_b, cw_all, cb_all, w_ffn_out_b, l, seq,
                  final_gain)

    return x2.reshape(bsz, seq, D_MODEL)
```

```python
import functools
import math

import jax
import jax.numpy as jnp
import numpy as np
from jax import lax
from jax.experimental import pallas as pl
from jax.experimental.pallas import tpu as pltpu

F32 = jnp.float32
BF16 = jnp.bfloat16

D_MODEL = 2048
DEPTH = 4
CHUNK = 64
D_DIFF = 1024
HEAD_DIM = 128
QK_HALF = 64
N_HEADS = 8
QK_SCALE = 1.0 / math.sqrt(QK_HALF)
LOG2E = 1.4426950408889634
D_POOL = 512
POOL_WINDOWS = (2, 4, 8, 16)
POOL_GROUP = 128
CONV_CH = 512
CONV_K = 31
D_IN = 3 * D_DIFF + D_POOL + 2 * CONV_CH
D_MIX = D_DIFF + D_POOL + CONV_CH
D_FF = 5504
FFN_CONV_K = 3
RMS_EPS = 1e-6
LN_EPS = 1e-5

VMEM_LIMIT_BYTES = 56 * 1024 * 1024
SUBLANES = 8

IN_TM, IN_TN = 1024, 1536
ATT_T = 512
ATT_NQ = 256
ATT_ONES = 16
MIX_TS = 512
MIX_HALO = 32
MIX_RC = 64
OUT_TM, OUT_TN = 512, 2048
FFN_TM, FFN_TF = 512, 512
FFN_TC = 256
D_FF_PAD = ((D_FF + FFN_TF - 1) // FFN_TF) * FFN_TF


def _params(*sem):
    return pltpu.CompilerParams(dimension_semantics=sem, vmem_limit_bytes=VMEM_LIMIT_BYTES)


def _rms(x, g):
    return x * lax.rsqrt(jnp.mean(x * x, axis=-1, keepdims=True) + RMS_EPS) * g


def _in_proj_kernel(x_ref, g_ref, w_ref, o_ref, h_ref):
    @pl.when(pl.program_id(1) == 0)
    def _():
        h_ref[...] = _rms(x_ref[...], g_ref[...]).astype(BF16)

    o_ref[...] = jnp.dot(h_ref[...], w_ref[...], preferred_element_type=F32).astype(o_ref.dtype)


def _in_proj(x2, g, w_all, layer):
    m = x2.shape[0]
    tm = min(IN_TM, m)
    return pl.pallas_call(
        _in_proj_kernel,
        grid=(m // tm, D_IN // IN_TN),
        in_specs=[
            pl.BlockSpec((tm, D_MODEL), lambda i, j: (i, 0)),
            pl.BlockSpec((1, D_MODEL), lambda i, j: (0, 0)),
            pl.BlockSpec((None, D_MODEL, IN_TN), lambda i, j: (layer, 0, j)),
        ],
        out_specs=pl.BlockSpec((tm, IN_TN), lambda i, j: (i, j)),
        out_shape=jax.ShapeDtypeStruct((m, D_IN), BF16),
        scratch_shapes=[pltpu.VMEM((tm, D_MODEL), BF16)],
        compiler_params=_params("parallel", "arbitrary"),
        name="in_proj",
    )(x2, g, w_all)


def _attn_tables(nb):
    qi, kj, first = [], [], []
    for i in range(nb):
        for idx, j in enumerate([i] + list(range(i))):
            qi.append(i)
            kj.append(j)
            first.append(1 if idx == 0 else 0)
    return np.asarray([qi, kj, first], np.int32)


def _attn_kernel(sc_ref, tab_ref, q_ref, k_ref, v_ref, pos_ref, gain_ref, o_ref,
                 kaug_ref, vt_ref, diag_ref, qa1_ref, qa2_ref, sa_ref, sb_ref,
                 m1_ref, acc1_ref, m2_ref, acc2_ref, *, n_blocks, n_steps, out_scale):
    t = ATT_T
    h = pl.program_id(1)
    lam = sc_ref[0]
    c_hi = sc_ref[1 + h]
    c_lo = sc_ref[1 + N_HEADS + h]
    row = lax.broadcasted_iota(jnp.int32, (HEAD_DIM, t), 0)

    for c in range(n_blocks):
        rows = pl.ds(c * t, t)
        kaug_ref[c, :, 0:HEAD_DIM] = k_ref[rows, :]
        kaug_ref[c, :, HEAD_DIM:2 * HEAD_DIM] = pos_ref[rows, :]
        vt_ref[c, 0:HEAD_DIM, :] = v_ref[rows, :].astype(F32).T.astype(BF16)
        vt_ref[c, HEAD_DIM:HEAD_DIM + ATT_ONES, :] = jnp.ones((ATT_ONES, t), BF16)
    for c in range(t // ATT_NQ):
        kk = lax.broadcasted_iota(jnp.int32, (t, ATT_NQ), 0)
        tt = lax.broadcasted_iota(jnp.int32, (t, ATT_NQ), 1) + c * ATT_NQ
        allowed = (kk // CHUNK) <= (tt // CHUNK)
        future = jnp.where(kk > tt, (2.0 * (c_hi + c_lo)) * (tt - kk).astype(F32), 0.0)
        diag_ref[c] = jnp.where(allowed, future, -jnp.inf)
    coef = jnp.where(row == 0, c_hi * 256.0,
                     jnp.where(row == 1, c_hi,
                               jnp.where(row == 2, c_lo * 256.0,
                                         jnp.where(row == 3, c_lo, 0.0)))).astype(BF16)
    qa1_ref[HEAD_DIM:2 * HEAD_DIM, :] = coef
    qa2_ref[HEAD_DIM:2 * HEAD_DIM, :] = coef

    def block_rows(qi):
        return pl.ds(pl.multiple_of(qi * t, t), t)

    def prep_q(qi):
        qt = (q_ref[block_rows(qi), :].astype(F32) * (QK_SCALE * LOG2E)).T
        qa1_ref[0:HEAD_DIM, :] = jnp.where(row < QK_HALF, qt, 0.0).astype(BF16)
        qa2_ref[0:HEAD_DIM, :] = jnp.where(row >= QK_HALF, qt, 0.0).astype(BF16)

    def init_stats():
        for m_ref, acc_ref in ((m1_ref, acc1_ref), (m2_ref, acc2_ref)):
            m_ref[...] = jnp.full((1, t), -jnp.inf, F32)
            acc_ref[...] = jnp.zeros((HEAD_DIM + ATT_ONES, t), F32)

    units = [(mp, c) for c in range(t // ATT_NQ) for mp in range(2)]
    qa_refs = (qa1_ref, qa2_ref)
    acc_refs = (acc1_ref, acc2_ref)
    m_refs = (m1_ref, m2_ref)

    def cols(c):
        return slice(c * ATT_NQ, (c + 1) * ATT_NQ)

    def scores(n, s_ref, mp, c):
        kb = kaug_ref[tab_ref[1, n]]
        s_ref[mp, c] = jnp.dot(kb, qa_refs[mp][:, cols(c)], preferred_element_type=F32)

    def softmax(s_ref, mp, c, diagonal):
        rows = slice(0, ATT_NQ * (c + 1)) if diagonal else slice(None)

        def biased():
            s = s_ref[mp, c, rows, :]
            return s + diag_ref[c, rows, :] if diagonal else s

        m_old = m_refs[mp][:, cols(c)]
        m_new = jnp.maximum(m_old, jnp.max(biased(), axis=0, keepdims=True))
        m_refs[mp][:, cols(c)] = m_new
        return jnp.exp2(m_old - m_new), jnp.exp2(biased() - m_new).astype(BF16)

    def weighted_values(n, mp, c, alpha, p):
        vt = vt_ref[tab_ref[1, n], :, 0:p.shape[0]]
        acc_refs[mp][:, cols(c)] = (alpha * acc_refs[mp][:, cols(c)]
                                    + jnp.dot(vt, p, preferred_element_type=F32))

    def compute(n, s_ref):
        for mp, c in units:
            scores(n, s_ref, mp, c)

    def consume(n, s_cur, s_next, diagonal):
        pending = None
        for mp, c in units:
            if s_next is not None:
                scores(n + 1, s_next, mp, c)
            ap = softmax(s_cur, mp, c, diagonal)
            if pending is not None:
                weighted_values(n, *pending)
            pending = (mp, c) + ap
        weighted_values(n, *pending)

    def finalize(qi):
        r1 = 1.0 / acc1_ref[HEAD_DIM:HEAD_DIM + 1, :]
        r2 = lam / acc2_ref[HEAD_DIM:HEAD_DIM + 1, :]
        o = acc1_ref[0:HEAD_DIM, :] * r1 - acc2_ref[0:HEAD_DIM, :] * r2
        o = o * lax.rsqrt(jnp.mean(o * o, axis=0, keepdims=True) + RMS_EPS)
        o_ref[block_rows(qi), :] = (o.T * gain_ref[...] * out_scale).astype(o_ref.dtype)

    def consume_step(n, s_cur, s_next):
        @pl.when(tab_ref[2, n] == 1)
        def _():
            init_stats()
            consume(n, s_cur, s_next, True)

        @pl.when(tab_ref[2, n] != 1)
        def _():
            consume(n, s_cur, s_next, False)

    def pipelined_step(n, s_cur, s_next):
        next_first = tab_ref[2, n + 1] == 1

        @pl.when(next_first)
        def _():
            prep_q(tab_ref[0, n + 1])

        consume_step(n, s_cur, s_next)

        @pl.when(next_first)
        def _():
            finalize(tab_ref[0, n])

    prep_q(0)
    compute(0, sa_ref)

    def body(m, carry):
        pipelined_step(2 * m, sa_ref, sb_ref)
        pipelined_step(2 * m + 1, sb_ref, sa_ref)
        return carry

    n_pairs = (n_steps - 1) // 2
    lax.fori_loop(0, n_pairs, body, 0)
    s_last = sa_ref
    if (n_steps - 1) % 2 == 1:
        pipelined_step(n_steps - 2, sa_ref, sb_ref)
        s_last = sb_ref
    consume_step(n_steps - 1, s_last, None)
    finalize(n_blocks - 1)


def _attention(p3, scalars, posfeat, gain, lam_init):
    b, s, _ = p3.shape
    t = ATT_T
    nb = s // t
    tables = _attn_tables(nb)
    kern = functools.partial(_attn_kernel, n_blocks=nb, n_steps=tables.shape[1], out_scale=1.0 - lam_init)
    hb = D_DIFF // HEAD_DIM
    return pl.pallas_call(
        kern,
        grid=(b, N_HEADS),
        in_specs=[
            pl.BlockSpec(memory_space=pltpu.SMEM),
            pl.BlockSpec(memory_space=pltpu.SMEM),
            pl.BlockSpec((None, s, HEAD_DIM), lambda b_, h: (b_, 0, h)),
            pl.BlockSpec((None, s, HEAD_DIM), lambda b_, h: (b_, 0, hb + h)),
            pl.BlockSpec((None, s, HEAD_DIM), lambda b_, h: (b_, 0, 2 * hb + h)),
            pl.BlockSpec((s, HEAD_DIM), lambda b_, h: (0, 0)),
            pl.BlockSpec((None, 1, HEAD_DIM), lambda b_, h: (h, 0, 0)),
        ],
        out_specs=pl.BlockSpec((None, s, HEAD_DIM), lambda b_, h: (b_, 0, h)),
        out_shape=jax.ShapeDtypeStruct((b, s, D_DIFF), BF16),
        scratch_shapes=[
            pltpu.VMEM((nb, t, 2 * HEAD_DIM), BF16),
            pltpu.VMEM((nb, HEAD_DIM + ATT_ONES, t), BF16),
            pltpu.VMEM((t // ATT_NQ, t, ATT_NQ), F32),
            pltpu.VMEM((2 * HEAD_DIM, t), BF16),
            pltpu.VMEM((2 * HEAD_DIM, t), BF16),
            pltpu.VMEM((2, t // ATT_NQ, t, ATT_NQ), F32),
            pltpu.VMEM((2, t // ATT_NQ, t, ATT_NQ), F32),
            pltpu.VMEM((1, t), F32), pltpu.VMEM((HEAD_DIM + ATT_ONES, t), F32),
            pltpu.VMEM((1, t), F32), pltpu.VMEM((HEAD_DIM + ATT_ONES, t), F32),
        ],
        compiler_params=_params("parallel", "parallel"),
        name="diff_attention",
    )(scalars, jnp.asarray(tables), p3, p3, p3, posfeat, gain)


def _mixer_kernel(a_ref, c1_ref, c2_ref, pw_ref, ps_ref, dww_ref, dwb_ref, lng_ref, lnb_ref,
                  cpw_ref, cpb_ref, o_ref, aext_ref, gext_ref, conv_ref):
    ts = MIX_TS
    hl = MIX_HALO
    si = pl.program_id(1)

    @pl.when(si == 0)
    def _():
        aext_ref[0:hl, :] = jnp.zeros((hl, D_POOL), F32)
        gext_ref[0:hl, :] = jnp.zeros((hl, CONV_CH), F32)

    @pl.when(si != 0)
    def _():
        aext_ref[0:hl, :] = aext_ref[ts:ts + hl, :]
        gext_ref[0:hl, :] = gext_ref[ts:ts + hl, :]

    aext_ref[hl:hl + ts, :] = a_ref[...].astype(F32)
    half_c1 = 0.5 * c1_ref[...].astype(F32)
    gext_ref[hl:hl + ts, :] = half_c1 + half_c1 * jnp.tanh(0.5 * c2_ref[...].astype(F32))

    tpos = si * ts + lax.broadcasted_iota(jnp.int32, (ts, 1), 0)
    for g, w in enumerate(POOL_WINDOWS):
        cols = slice(g * POOL_GROUP, (g + 1) * POOL_GROUP)
        x = aext_ref[:, cols]
        acc, span = x, 1
        while span < w:
            acc = acc + pltpu.roll(acc, span, 0)
            span *= 2
        cnt = jnp.minimum(tpos + 1, w).astype(F32)
        pooled = acc[hl:, :] / cnt - x[hl:, :]
        y = jnp.dot(pooled.astype(BF16), pw_ref[g], preferred_element_type=F32)
        o_ref[:, cols] = (y * ps_ref[:, cols]).astype(o_ref.dtype)

    rc = MIX_RC
    for r0 in range(0, ts, rc):
        for c0 in range(0, CONV_CH, 128):
            cols = slice(c0, c0 + 128)
            acc = jnp.zeros((rc, 128), F32) + dwb_ref[:, cols]
            for b in range(SUBLANES):
                part = None
                for a8 in range((CONV_K - 1 - b) // SUBLANES + 1):
                    d = SUBLANES * a8 + b
                    lo = hl + r0 - SUBLANES * a8 - (SUBLANES if b else 0)
                    win = gext_ref[lo:hl + r0 - SUBLANES * a8 + rc, cols]
                    term = dww_ref[CONV_K - 1 - d:CONV_K - d, cols] * win
                    part = term if part is None else part + term
                acc = acc + (pltpu.roll(part, b, 0)[SUBLANES:, :] if b else part)
            conv_ref[r0:r0 + rc, cols] = acc

    u = conv_ref[...]
    mu = jnp.mean(u, axis=-1, keepdims=True)
    var = jnp.mean(jnp.square(u - mu), axis=-1, keepdims=True)
    hz = (u - mu) * lax.rsqrt(var + LN_EPS) * lng_ref[...] + lnb_ref[...]
    z = hz + hz * jnp.tanh(hz)
    y = jnp.dot(z.astype(BF16), cpw_ref[...], preferred_element_type=F32) + cpb_ref[...]
    o_ref[:, D_POOL:D_POOL + CONV_CH] = y.astype(o_ref.dtype)


def _mixer(p3, pool_w, pool_scale, dw_w, dw_b, ln_g, ln_b, pw_w, pw_b):
    b, s, _ = p3.shape
    ts = MIX_TS
    a_blk = (3 * D_DIFF) // D_POOL
    full = lambda shape: pl.BlockSpec(shape, lambda b_, i: (0,) * len(shape))
    return pl.pallas_call(
        _mixer_kernel,
        grid=(b, s // ts),
        in_specs=[
            pl.BlockSpec((None, ts, D_POOL), lambda b_, i: (b_, i, a_blk)),
            pl.BlockSpec((None, ts, CONV_CH), lambda b_, i: (b_, i, a_blk + 1)),
            pl.BlockSpec((None, ts, CONV_CH), lambda b_, i: (b_, i, a_blk + 2)),
            full((len(POOL_WINDOWS), POOL_GROUP, POOL_GROUP)),
            full((1, D_POOL)),
            full((CONV_K, CONV_CH)),
            full((1, CONV_CH)),
            full((1, CONV_CH)),
            full((1, CONV_CH)),
            full((CONV_CH, CONV_CH)),
            full((1, CONV_CH)),
        ],
        out_specs=pl.BlockSpec((None, ts, D_POOL + CONV_CH), lambda b_, i: (b_, i, 0)),
        out_shape=jax.ShapeDtypeStruct((b, s, D_POOL + CONV_CH), BF16),
        scratch_shapes=[
            pltpu.VMEM((MIX_HALO + ts, D_POOL), F32),
            pltpu.VMEM((MIX_HALO + ts, CONV_CH), F32),
            pltpu.VMEM((ts, CONV_CH), F32),
        ],
        compiler_params=_params("parallel", "arbitrary"),
        name="pool_conformer",
    )(p3, p3, p3, pool_w, pool_scale, dw_w, dw_b, ln_g, ln_b, pw_w, pw_b)


def _out_proj_kernel(x_ref, yb_ref, yac_ref, wb_ref, wac_ref, o_ref):
    acc = jnp.dot(yb_ref[...], wb_ref[...], preferred_element_type=F32)
    acc = acc + jnp.dot(yac_ref[...], wac_ref[...], preferred_element_type=F32)
    o_ref[...] = x_ref[...] + acc


def _out_proj(x2, yb, yac, w_all, layer):
    m = x2.shape[0]
    tm = min(OUT_TM, m)
    tn = OUT_TN
    kb = D_DIFF
    return pl.pallas_call(
        _out_proj_kernel,
        grid=(m // tm, D_MODEL // tn),
        in_specs=[
            pl.BlockSpec((tm, tn), lambda i, j: (i, j)),
            pl.BlockSpec((tm, kb), lambda i, j: (i, 0)),
            pl.BlockSpec((tm, D_MIX - kb), lambda i, j: (i, 0)),
            pl.BlockSpec((None, kb, tn), lambda i, j: (layer, 0, j)),
            pl.BlockSpec((None, D_MIX - kb, tn), lambda i, j: (layer, 1, j)),
        ],
        out_specs=pl.BlockSpec((tm, tn), lambda i, j: (i, j)),
        out_shape=jax.ShapeDtypeStruct((m, D_MODEL), F32),
        compiler_params=_params("parallel", "arbitrary"),
        name="out_proj",
    )(x2, yb, yac, w_all, w_all)


def _ffn_kernel(*refs, tiles_per_seq, n_tiles, final_norm):
    x_ref, g_ref, wg_ref, wu_ref, cw_ref, cb_ref, wo_ref = refs[:7]
    fg_ref = refs[7] if final_norm else None
    o_ref, h_ref, gext_ref, halo_ref = refs[-4:]
    tm = FFN_TM
    mi = pl.program_id(0)
    f = pl.program_id(1)

    @pl.when(f == 0)
    def _():
        x = x_ref[...]
        h_ref[...] = _rms(x, g_ref[...]).astype(BF16)
        o_ref[...] = x

    first = (mi % tiles_per_seq) == 0

    @pl.when(first)
    def _():
        gext_ref[:, 0:SUBLANES, :] = jnp.zeros((FFN_TF // FFN_TC, SUBLANES, FFN_TC), F32)

    @pl.when(jnp.logical_not(first))
    def _():
        gext_ref[:, 0:SUBLANES, :] = halo_ref[f]

    h = h_ref[...]
    out = None
    for c in range(FFN_TF // FFN_TC):
        cols = slice(c * FFN_TC, (c + 1) * FFN_TC)
        gate = jnp.dot(h, wg_ref[:, cols], preferred_element_type=F32)
        up = jnp.dot(h, wu_ref[:, cols], preferred_element_type=F32)
        gext_ref[c, SUBLANES:SUBLANES + tm, :] = gate
        halo_ref[f, c] = gate[tm - SUBLANES:, :]
        gx = gext_ref[c]
        g1 = pltpu.roll(gx, 1, 0)[SUBLANES:, :]
        g2 = pltpu.roll(gx, 2, 0)[SUBLANES:, :]
        hg = cw_ref[0:1, cols] * g2 + cw_ref[1:2, cols] * g1 + cw_ref[2:3, cols] * gate + cb_ref[:, cols]
        act = ((hg + hg * jnp.tanh(hg)) * up).astype(BF16)
        part = jnp.dot(act, wo_ref[cols, :], preferred_element_type=F32)
        out = part if out is None else out + part
    o_ref[...] += out

    if final_norm:
        @pl.when(f == n_tiles - 1)
        def _():
            o_ref[...] = _rms(o_ref[...], fg_ref[...])


def _ffn(x2, g, w_gate_all, w_up_all, cw_all, cb_all, w_out_all, layer, seq, final_gain=None):
    m = x2.shape[0]
    tm, tf = FFN_TM, FFN_TF
    nf = D_FF_PAD // tf
    kern = functools.partial(_ffn_kernel, tiles_per_seq=seq // tm, n_tiles=nf, final_norm=final_gain is not None)
    in_specs = [
        pl.BlockSpec((tm, D_MODEL), lambda i, j: (i, 0)),
        pl.BlockSpec((1, D_MODEL), lambda i, j: (0, 0)),
        pl.BlockSpec((None, D_MODEL, tf), lambda i, j: (layer, 0, j)),
        pl.BlockSpec((None, D_MODEL, tf), lambda i, j: (layer, 0, j)),
        pl.BlockSpec((None, FFN_CONV_K, tf), lambda i, j: (layer, 0, j)),
        pl.BlockSpec((None, 1, tf), lambda i, j: (layer, 0, j)),
        pl.BlockSpec((None, tf, D_MODEL), lambda i, j: (layer, j, 0)),
    ]
    args = [x2, g, w_gate_all, w_up_all, cw_all, cb_all, w_out_all]
    if final_gain is not None:
        in_specs.append(pl.BlockSpec((1, D_MODEL), lambda i, j: (0, 0)))
        args.append(final_gain)
    return pl.pallas_call(
        kern,
        grid=(m // tm, nf),
        in_specs=in_specs,
        out_specs=pl.BlockSpec((tm, D_MODEL), lambda i, j: (i, 0)),
        out_shape=jax.ShapeDtypeStruct((m, D_MODEL), F32),
        scratch_shapes=[
            pltpu.VMEM((tm, D_MODEL), BF16),
            pltpu.VMEM((tf // FFN_TC, SUBLANES + tm, FFN_TC), F32),
            pltpu.VMEM((nf, tf // FFN_TC, SUBLANES, FFN_TC), F32),
        ],
        compiler_params=_params("arbitrary", "arbitrary"),
        name="convglu_ffn",
    )(*args)


def kernel(x, norm_mix, w_in, lam_q1, lam_k1, lam_q2, lam_k2, diff_head_gain, pool_w, pool_scale,
           conv_dw_w, conv_dw_b, conv_ln_g, conv_ln_b, conv_pw_w, conv_pw_b, w_out, norm_ffn,
           w_ffn_in, ffn_conv_w, ffn_conv_b, w_ffn_out, final_norm):
    bsz, seq, _ = x.shape
    m = bsz * seq
    assert seq % MIX_TS == 0 and seq % ATT_T == 0 and seq % FFN_TM == 0 and seq <= 256 * 256

    coef = np.asarray([2.0 ** (-8.0 * (h + 1.0) / N_HEADS) * LOG2E for h in range(N_HEADS)], np.float64)
    coef_hi = coef.astype(BF16).astype(np.float64)
    coef_lo = (coef - coef_hi).astype(BF16).astype(np.float64)
    coefs = jnp.asarray(np.concatenate([coef_hi, coef_lo]), F32)
    kpos = np.arange(seq)
    posfeat = np.zeros((seq, HEAD_DIM), np.float32)
    posfeat[:, 0] = posfeat[:, 2] = kpos // 256
    posfeat[:, 1] = posfeat[:, 3] = kpos % 256
    posfeat = jnp.asarray(posfeat, BF16)

    ff_pad = D_FF_PAD - D_FF
    w_in_b = w_in.astype(BF16)
    w_out_b = w_out.astype(BF16)
    w_gate_b = jnp.pad(w_ffn_in[:, :, :D_FF].astype(BF16), ((0, 0), (0, 0), (0, ff_pad)))
    w_up_b = jnp.pad(w_ffn_in[:, :, D_FF:].astype(BF16), ((0, 0), (0, 0), (0, ff_pad)))
    w_ffn_out_b = jnp.pad(w_ffn_out.astype(BF16), ((0, 0), (0, ff_pad), (0, 0)))
    cw_all = jnp.pad(0.5 * ffn_conv_w, ((0, 0), (0, 0), (0, ff_pad)))
    cb_all = jnp.pad(0.5 * ffn_conv_b, ((0, 0), (0, ff_pad))).reshape(DEPTH, 1, D_FF_PAD)

    x2 = x.reshape(m, D_MODEL)
    for l in range(DEPTH):
        lam_init = 0.8 - 0.6 * math.exp(-0.3 * l)
        lam = (jnp.exp(jnp.sum(lam_q1[l] * lam_k1[l])) - jnp.exp(jnp.sum(lam_q2[l] * lam_k2[l])) + lam_init)
        scalars = jnp.concatenate([lam.reshape(1).astype(F32), coefs])

        p = _in_proj(x2, norm_mix[l].reshape(1, D_MODEL), w_in_b, l)
        p3 = p.reshape(bsz, seq, D_IN)
        y_b = _attention(p3, scalars, posfeat, diff_head_gain[l].reshape(N_HEADS, 1, HEAD_DIM), lam_init)
        y_ac = _mixer(p3, pool_w[l].astype(BF16), pool_scale[l].reshape(1, D_POOL), conv_dw_w[l],
                      conv_dw_b[l].reshape(1, CONV_CH), 0.5 * conv_ln_g[l].reshape(1, CONV_CH),
                      0.5 * conv_ln_b[l].reshape(1, CONV_CH), conv_pw_w[l].astype(BF16),
                      conv_pw_b[l].reshape(1, CONV_CH))
        x2 = _out_proj(x2, y_b.reshape(m, D_DIFF), y_ac.reshape(m, D_POOL + CONV_CH), w_out_b, l)

        final_gain = final_norm.reshape(1, D_MODEL) if l == DEPTH - 1 else None
        x2 = _ffn(x2, norm_ffn[l].reshape(1, D_MODEL), w_gate_b, w_up_b, cw_all, cb_all, w_ffn_out_b, l, seq,
                  final_gain)

    return x2.reshape(bsz, seq, D_MODEL)
```
